```python
import math
import jax, jax.numpy as jnp
from jax import lax
import numpy as np

D_MODEL = 4096
BATCH = 1
SEQ = 16384
DEPTH = 4

HEAD_DIM = 128
A_HEADS = 8
B_GROUPS = 8
C_GROUPS = 8
W_A = A_HEADS * HEAD_DIM
W_B = B_GROUPS * HEAD_DIM
W_C = C_GROUPS * HEAD_DIM
D_MIX = W_A + W_B + W_C
D_IN = 2 * W_A + 2 * W_B + 3 * W_C
CHUNK = 128
CONF_K = 31
SC_K = 3
N_MEM = 256
X_HEADS = 4
X_HEAD_DIM = 256
X_WIDTH = X_HEADS * X_HEAD_DIM
D_FF = ((8 * D_MODEL // 3 + 255) // 256) * 256
EPS = 1e-6

kernel_name = "hybrid_gmlp_conformer_shortconv_xattn"


def rms_norm(x, g):
    xf = x.astype(jnp.float32)
    y = xf * lax.rsqrt(jnp.mean(xf * xf, axis=-1, keepdims=True) + EPS)
    return (y * g.astype(jnp.float32)).astype(x.dtype)


def layer_norm(x, g, b):
    xf = x.astype(jnp.float32)
    mu = jnp.mean(xf, axis=-1, keepdims=True)
    xc = xf - mu
    y = xc * lax.rsqrt(jnp.mean(xc * xc, axis=-1, keepdims=True) + EPS)
    return (y * g.astype(jnp.float32) + b.astype(jnp.float32)).astype(x.dtype)


def causal_depthwise_conv(x, w):
    k, c = w.shape
    return lax.conv_general_dilated(
        x, w[:, None, :].astype(x.dtype), window_strides=(1,),
        padding=[(k - 1, 0)], dimension_numbers=("NWC", "WIO", "NWC"),
        feature_group_count=c)


def gmlp_spatial_gating(z, ws, bs, ln_g, ln_b):
    z = jax.nn.gelu(z)
    u, v = jnp.split(z, 2, axis=-1)
    v = layer_norm(v, ln_g, ln_b)
    bsz, s, _ = v.shape
    vh = v.reshape(bsz, s // CHUNK, CHUNK, A_HEADS, HEAD_DIM)
    causal = jnp.tril(jnp.ones((CHUNK, CHUNK), dtype=bool))
    wm = jnp.where(causal[None], ws, jnp.zeros_like(ws)).astype(v.dtype)
    mixed = jnp.einsum("hts,bcshd->bcthd", wm, vh) + bs.T.astype(v.dtype)[None, None, :, :, None]
    return u * mixed.reshape(bsz, s, W_A)


def conformer_conv(z, conv_w, conv_b, ng, nb):
    a, g = jnp.split(z, 2, axis=-1)
    h = a * jax.nn.sigmoid(g)
    h = causal_depthwise_conv(h, conv_w) + conv_b.astype(h.dtype)
    h = layer_norm(h, ng, nb)
    return jax.nn.silu(h)


def short_gated_conv(z, sc_w):
    b, c, h = jnp.split(z, 3, axis=-1)
    return b * causal_depthwise_conv(c * h, sc_w)


def memory_cross_attention(h, mem_n, wq, wkv, wo):
    bsz, s, _ = h.shape
    q = (h @ wq).reshape(bsz, s, X_HEADS, X_HEAD_DIM)
    k, v = jnp.split(mem_n @ wkv, 2, axis=-1)
    k = k.reshape(bsz, N_MEM, X_HEADS, X_HEAD_DIM)
    v = v.reshape(bsz, N_MEM, X_HEADS, X_HEAD_DIM)
    scores = jnp.einsum("bshd,bmhd->bhsm", q, k).astype(jnp.float32) * (X_HEAD_DIM ** -0.5)
    p = jax.nn.softmax(scores, axis=-1).astype(h.dtype)
    o = jnp.einsum("bhsm,bmhd->bshd", p, v).reshape(bsz, s, X_WIDTH)
    return o @ wo


def swiglu(h, w_gu, w_down):
    gate, up = jnp.split(h @ w_gu, 2, axis=-1)
    return (jax.nn.silu(gate) * up) @ w_down


def setup_inputs(seed: int = 0) -> dict:
    key = jax.random.key(seed)
    ks = jax.random.split(key, 32)
    f32 = jnp.float32

    def nrm(k, shape, scale):
        return jax.random.normal(k, shape, f32) * scale

    def gain(k, shape):
        return 1.0 + 0.05 * jax.random.normal(k, shape, f32)

    L = DEPTH
    return {
        "x": jax.random.normal(ks[0], (BATCH, SEQ, D_MODEL), f32),
        "mem": jax.random.normal(ks[1], (BATCH, N_MEM, D_MODEL), f32),
        "norm_mix_pre": gain(ks[2], (L, D_MODEL)),
        "norm_mix_post": gain(ks[3], (L, D_MODEL)),
        "w_in": nrm(ks[4], (L, D_MODEL, D_IN), D_MODEL ** -0.5),
        "gmlp_ws": nrm(ks[5], (L, A_HEADS, CHUNK, CHUNK), 0.5 * CHUNK ** -0.5),
        "gmlp_b": gain(ks[6], (L, A_HEADS, CHUNK)),
        "gmlp_ln_g": gain(ks[7], (L, W_A)),
        "gmlp_ln_b": nrm(ks[8], (L, W_A), 0.02),
        "conf_w": nrm(ks[9], (L, CONF_K, W_B), CONF_K ** -0.5),
        "conf_b": nrm(ks[10], (L, W_B), 0.02),
        "conf_ln_g": gain(ks[11], (L, W_B)),
        "conf_ln_b": nrm(ks[12], (L, W_B), 0.02),
        "sc_w": nrm(ks[13], (L, SC_K, W_C), SC_K ** -0.5),
        "w_out": nrm(ks[14], (L, D_MIX, D_MODEL), D_MIX ** -0.5),
        "norm_x_pre": gain(ks[15], (L, D_MODEL)),
        "norm_x_post": gain(ks[16], (L, D_MODEL)),
        "norm_mem": gain(ks[17], (L, D_MODEL)),
        "wq": nrm(ks[18], (L, D_MODEL, X_WIDTH), D_MODEL ** -0.5),
        "wkv": nrm(ks[19], (L, D_MODEL, 2 * X_WIDTH), D_MODEL ** -0.5),
        "wo": nrm(ks[20], (L, X_WIDTH, D_MODEL), X_WIDTH ** -0.5),
        "norm_ffn_pre": gain(ks[21], (L, D_MODEL)),
        "norm_ffn_post": gain(ks[22], (L, D_MODEL)),
        "w_gu": nrm(ks[23], (L, D_MODEL, 2 * D_FF), D_MODEL ** -0.5),
        "w_down": nrm(ks[24], (L, D_FF, D_MODEL), D_FF ** -0.5),
    }


def reference(x, mem, norm_mix_pre, norm_mix_post, w_in, gmlp_ws, gmlp_b, gmlp_ln_g,
              gmlp_ln_b, conf_w, conf_b, conf_ln_g, conf_ln_b, sc_w, w_out,
              norm_x_pre, norm_x_post, norm_mem, wq, wkv, wo,
              norm_ffn_pre, norm_ffn_post, w_gu, w_down):
    split_idx = [2 * W_A, 2 * W_A + 2 * W_B]
    for l in range(DEPTH):
        h = rms_norm(x, norm_mix_pre[l])
        z = h @ w_in[l]
        z_a, z_b, z_c = jnp.split(z, split_idx, axis=-1)
        y_a = gmlp_spatial_gating(z_a, gmlp_ws[l], gmlp_b[l], gmlp_ln_g[l], gmlp_ln_b[l])
        y_b = conformer_conv(z_b, conf_w[l], conf_b[l], conf_ln_g[l], conf_ln_b[l])
        y_c = short_gated_conv(z_c, sc_w[l])
        y = jnp.concatenate([y_a, y_b, y_c], axis=-1) @ w_out[l]
        x = x + rms_norm(y, norm_mix_post[l])
        h = rms_norm(x, norm_x_pre[l])
        mem_n = rms_norm(mem, norm_mem[l])
        y = memory_cross_attention(h, mem_n, wq[l], wkv[l], wo[l])
        x = x + rms_norm(y, norm_x_post[l])
        h = rms_norm(x, norm_ffn_pre[l])
        y = swiglu(h, w_gu[l], w_down[l])
        x = x + rms_norm(y, norm_ffn_post[l])
    return x
```

```python
import functools
import math

import jax
import jax.numpy as jnp
from jax import lax
from jax.experimental import pallas as pl
from jax.experimental.pallas import tpu as pltpu

EPS = 1e-6
CHUNK = 128
HEAD_DIM = 128
CONF_K = 31
SC_K = 3
X_HEADS = 4
SUBLANES = 8
CONF_HALO = 32
SC_HALO = 8
ROWS = 32
CONV_LANES = 256
FF_TILE = 512
VMEM_LIMIT = 60 * 1024 * 1024

BF16 = jnp.bfloat16
F32 = jnp.float32


def _params(sem):
    return pltpu.CompilerParams(dimension_semantics=sem, vmem_limit_bytes=VMEM_LIMIT)


def _resident(shape):
    zeros = (0,) * len(shape)
    return pl.BlockSpec(shape, lambda *_: zeros, pipeline_mode=pl.Buffered(1))


def _rms_scale(v):
    return lax.rsqrt(jnp.mean(v * v, axis=-1, keepdims=True) + EPS)


def _layer_norm(v, g, b):
    mu = jnp.mean(v, axis=-1, keepdims=True)
    vc = v - mu
    return vc * lax.rsqrt(jnp.mean(vc * vc, axis=-1, keepdims=True) + EPS) * g + b


def _sigmoid(v):
    return 1.0 / (1.0 + jnp.exp(-v))


def _gelu_tanh(v):
    c = math.sqrt(2.0 / math.pi)
    return 0.5 * v * (1.0 + jnp.tanh(c * (v + 0.044715 * (v * v * v))))


def _row_pieces(n_rows, body):
    def step(r, carry):
        body(pl.ds(pl.multiple_of(r * ROWS, ROWS), ROWS))
        return carry
    lax.fori_loop(0, n_rows // ROWS, step, 0)


def _prenorm_kernel(x_ref, g_ref, h_ref, *, tm):
    g = g_ref[...]

    def piece(rows):
        x = x_ref[rows, :]
        h_ref[rows, :] = (x * _rms_scale(x) * g).astype(h_ref.dtype)

    _row_pieces(tm, piece)


def _prenorm(x, g, tm=512):
    m, d = x.shape
    return pl.pallas_call(
        functools.partial(_prenorm_kernel, tm=tm),
        grid=(m // tm,),
        in_specs=[pl.BlockSpec((tm, d), lambda i: (i, 0)), _resident((1, d))],
        out_specs=pl.BlockSpec((tm, d), lambda i: (i, 0)),
        out_shape=jax.ShapeDtypeStruct((m, d), BF16),
        compiler_params=_params(("arbitrary",)),
        name="prenorm",
    )(x, g)


def _proj_kernel(a_ref, w_ref, o_ref):
    o_ref[...] = jnp.dot(a_ref[...], w_ref[...], preferred_element_type=F32).astype(o_ref.dtype)


def _proj(a, w, tm, tn, name):
    m, k = a.shape
    n = w.shape[1]
    return pl.pallas_call(
        _proj_kernel,
        grid=(m // tm, n // tn),
        in_specs=[pl.BlockSpec((tm, k), lambda i, j: (i, 0)),
                  pl.BlockSpec((k, tn), lambda i, j: (0, j))],
        out_specs=pl.BlockSpec((tm, tn), lambda i, j: (i, j)),
        out_shape=jax.ShapeDtypeStruct((m, n), BF16),
        compiler_params=_params(("arbitrary", "arbitrary")),
        name=name,
    )(a, w)


def _residual_tail(y_ref, x_ref, gpost_ref, gnext_ref, h_ref, n_rows):
    gpost = gpost_ref[...]
    gnext = gnext_ref[...]

    def piece(rows):
        y = y_ref[rows, :]
        xn = x_ref[rows, :] + y * _rms_scale(y) * gpost
        y_ref[rows, :] = xn
        h_ref[rows, :] = (xn * _rms_scale(xn) * gnext).astype(h_ref.dtype)

    _row_pieces(n_rows, piece)


def _causal_conv_piece(src_sc, w_ref, n_taps, halo, base, lanes, init):
    win = src_sc[pl.ds(base, halo + ROWS), lanes]
    acc = init
    for r in range(min(SUBLANES, n_taps)):
        shifted = win if r == 0 else pltpu.roll(win, r, axis=0)
        for q in range((n_taps - 1 - r) // SUBLANES + 1):
            k = n_taps - 1 - (SUBLANES * q + r)
            lo = halo - SUBLANES * q
            term = w_ref[k:k + 1, lanes] * shifted[lo:lo + ROWS, :]
            acc = term if acc is None else acc + term
    return acc


def _mixer_kernel(z_ref, x_ref, wout_ref, ws_ref, bsT_ref, lng_ref, lnb_ref,
                  cw_ref, cb_ref, cng_ref, cnb_ref, scw_ref, gpost_ref, gnext_ref,
                  xo_ref, h_ref,
                  wm_sc, v_sc, hb_sc, yb_sc, hc_sc, cat_sc, *, tm, wa, wb, wc):
    i = pl.program_id(0)
    heads = wa // HEAD_DIM
    zb0 = 2 * wa
    zc0 = 2 * wa + 2 * wb

    @pl.when(i == 0)
    def _init():
        hb_sc[0:CONF_HALO, :] = jnp.zeros((CONF_HALO, wb), F32)
        hc_sc[0:SC_HALO, :] = jnp.zeros((SC_HALO, wc), F32)
        row = lax.broadcasted_iota(jnp.int32, (CHUNK, CHUNK), 0)
        col = lax.broadcasted_iota(jnp.int32, (CHUNK, CHUNK), 1)
        for hh in range(heads):
            wm_sc[hh] = jnp.where(col <= row, ws_ref[hh], 0.0).astype(BF16)

    lng = lng_ref[...]
    lnb = lnb_ref[...]

    def norm_v(rows):
        v = _gelu_tanh(z_ref[rows, wa:2 * wa].astype(F32))
        v_sc[rows, :] = _layer_norm(v, lng, lnb).astype(BF16)

    _row_pieces(tm, norm_v)
    for c in range(tm // CHUNK):
        rows = slice(c * CHUNK, (c + 1) * CHUNK)
        for hh in range(heads):
            cols = slice(hh * HEAD_DIM, (hh + 1) * HEAD_DIM)
            mixed = jnp.dot(wm_sc[hh], v_sc[rows, cols], preferred_element_type=F32)
            mixed = mixed + bsT_ref[:, hh:hh + 1]
            u = _gelu_tanh(z_ref[rows, cols].astype(F32))
            cat_sc[rows, cols] = (u * mixed).astype(BF16)

    def glu(rows):
        a = z_ref[rows, zb0:zb0 + wb].astype(F32)
        g = z_ref[rows, zb0 + wb:zb0 + 2 * wb].astype(F32)
        dst = pl.ds(pl.multiple_of(rows.start + CONF_HALO, SUBLANES), ROWS)
        hb_sc[dst, :] = a * _sigmoid(g)

    _row_pieces(tm, glu)

    def conv_b(rows):
        for lb in range(wb // CONV_LANES):
            lanes = slice(lb * CONV_LANES, (lb + 1) * CONV_LANES)
            init = jnp.broadcast_to(cb_ref[:, lanes], (ROWS, CONV_LANES))
            yb_sc[rows, lanes] = _causal_conv_piece(
                hb_sc, cw_ref, CONF_K, CONF_HALO, rows.start, lanes, init)

    _row_pieces(tm, conv_b)
    hb_sc[0:CONF_HALO, :] = hb_sc[tm:tm + CONF_HALO, :]
    cng = cng_ref[...]
    cnb = cnb_ref[...]

    def norm_b(rows):
        yb = _layer_norm(yb_sc[rows, :], cng, cnb)
        cat_sc[rows, wa:wa + wb] = (yb * _sigmoid(yb)).astype(BF16)

    _row_pieces(tm, norm_b)

    def gate_c(rows):
        cc = z_ref[rows, zc0 + wc:zc0 + 2 * wc].astype(F32)
        hh_ = z_ref[rows, zc0 + 2 * wc:zc0 + 3 * wc].astype(F32)
        dst = pl.ds(pl.multiple_of(rows.start + SC_HALO, SUBLANES), ROWS)
        hc_sc[dst, :] = cc * hh_

    _row_pieces(tm, gate_c)

    def conv_c(rows):
        for lb in range(wc // CONV_LANES):
            lanes = slice(lb * CONV_LANES, (lb + 1) * CONV_LANES)
            conv = _causal_conv_piece(hc_sc, scw_ref, SC_K, SC_HALO, rows.start, lanes, None)
            bgate = z_ref[rows, zc0 + lb * CONV_LANES:zc0 + (lb + 1) * CONV_LANES].astype(F32)
            cat_sc[rows, wa + wb + lb * CONV_LANES:wa + wb + (lb + 1) * CONV_LANES] = (
                bgate * conv).astype(BF16)

    _row_pieces(tm, conv_c)
    hc_sc[0:SC_HALO, :] = hc_sc[tm:tm + SC_HALO, :]

    xo_ref[...] = jnp.dot(cat_sc[...], wout_ref[...], preferred_element_type=F32)
    _residual_tail(xo_ref, x_ref, gpost_ref, gnext_ref, h_ref, tm)


def _mixer(z, x, w_out, ws, bsT, lng, lnb, cw, cb, cng, cnb, scw, gpost, gnext, tm=128):
    m, d = x.shape
    d_in = z.shape[1]
    wa = ws.shape[0] * HEAD_DIM
    wb = cw.shape[1]
    wc = scw.shape[1]
    assert d_in == 2 * wa + 2 * wb + 3 * wc and w_out.shape[0] == wa + wb + wc
    assert cw.shape[0] == CONF_K and scw.shape[0] == SC_K and tm % CHUNK == 0
    row = lambda i: (i, 0)
    small = [ws, bsT, lng, lnb, cw, cb, cng, cnb, scw, gpost, gnext]
    return pl.pallas_call(
        functools.partial(_mixer_kernel, tm=tm, wa=wa, wb=wb, wc=wc),
        grid=(m // tm,),
        in_specs=[pl.BlockSpec((tm, d_in), row), pl.BlockSpec((tm, d), row),
                  _resident(w_out.shape)] + [_resident(a.shape) for a in small],
        out_specs=[pl.BlockSpec((tm, d), row), pl.BlockSpec((tm, d), row)],
        out_shape=[jax.ShapeDtypeStruct((m, d), F32), jax.ShapeDtypeStruct((m, d), BF16)],
        scratch_shapes=[pltpu.VMEM(ws.shape, BF16),
                        pltpu.VMEM((tm, wa), BF16),
                        pltpu.VMEM((CONF_HALO + tm, wb), F32),
                        pltpu.VMEM((tm, wb), F32),
                        pltpu.VMEM((SC_HALO + tm, wc), F32),
                        pltpu.VMEM((tm, wa + wb + wc), BF16)],
        compiler_params=_params(("arbitrary",)),
        name="mixer",
    )(z, x, w_out, *small)


def _kv_kernel(mem_ref, g_ref, w_ref, o_ref):
    mem = mem_ref[...]
    mn = (mem * _rms_scale(mem) * g_ref[0]).astype(BF16)
    o_ref[0] = jnp.dot(mn, w_ref[0], preferred_element_type=F32).astype(o_ref.dtype)


def _memory_kv(mem, g_mem, wkv, tn=1024):
    n_mem, d = mem.shape
    layers, _, n = wkv.shape
    return pl.pallas_call(
        _kv_kernel,
        grid=(layers, n // tn),
        in_specs=[pl.BlockSpec((n_mem, d), lambda l, j: (0, 0)),
                  pl.BlockSpec((1, 1, d), lambda l, j: (l, 0, 0)),
                  pl.BlockSpec((1, d, tn), lambda l, j: (l, 0, j))],
        out_specs=pl.BlockSpec((1, n_mem, tn), lambda l, j: (l, 0, j)),
        out_shape=jax.ShapeDtypeStruct((layers, n_mem, n), BF16),
        compiler_params=_params(("arbitrary", "arbitrary")),
        name="memory_kv",
    )(mem, g_mem, wkv)


def _attn_kernel(q_ref, kT_ref, v_ref, wo_ref, x_ref, gpost_ref, gnext_ref,
                 xo_ref, h_ref, o_sc, *, tm, dh):
    scale = dh ** -0.5
    for hd in range(X_HEADS):
        cols = slice(hd * dh, (hd + 1) * dh)
        s = jnp.dot(q_ref[:, cols], kT_ref[cols, :], preferred_element_type=F32) * scale
        p = jnp.exp(s - jnp.max(s, axis=-1, keepdims=True))
        p = p / jnp.sum(p, axis=-1, keepdims=True)
        o = jnp.dot(p.astype(BF16), v_ref[:, cols], preferred_element_type=F32)
        o_sc[:, cols] = o.astype(BF16)
    xo_ref[...] = jnp.dot(o_sc[...], wo_ref[...], preferred_element_type=F32)
    _residual_tail(xo_ref, x_ref, gpost_ref, gnext_ref, h_ref, tm)


def _attention(q, kT, v, wo, x, gpost, gnext, tm=256):
    m, d = x.shape
    xw = q.shape[1]
    row = lambda i: (i, 0)
    return pl.pallas_call(
        functools.partial(_attn_kernel, tm=tm, dh=xw // X_HEADS),
        grid=(m // tm,),
        in_specs=[pl.BlockSpec((tm, xw), row), _resident(kT.shape), _resident(v.shape),
                  _resident(wo.shape), pl.BlockSpec((tm, d), row),
                  _resident(gpost.shape), _resident(gnext.shape)],
        out_specs=[pl.BlockSpec((tm, d), row), pl.BlockSpec((tm, d), row)],
        out_shape=[jax.ShapeDtypeStruct((m, d), F32), jax.ShapeDtypeStruct((m, d), BF16)],
        scratch_shapes=[pltpu.VMEM((tm, xw), BF16)],
        compiler_params=_params(("arbitrary",)),
        name="attention",
    )(q, kT, v, wo, x, gpost, gnext)


def _ffn_up_kernel(h_ref, wg_ref, wu_ref, a_ref):
    h = h_ref[...]
    gate = jnp.dot(h, wg_ref[...], preferred_element_type=F32)
    up = jnp.dot(h, wu_ref[...], preferred_element_type=F32)
    a_ref[...] = (gate * _sigmoid(gate) * up).astype(a_ref.dtype)


def _ffn_up(h, wg, wu, tm=1024, tn=FF_TILE):
    m, d = h.shape
    n = wg.shape[1]
    return pl.pallas_call(
        _ffn_up_kernel,
        grid=(m // tm, n // tn),
        in_specs=[pl.BlockSpec((tm, d), lambda i, j: (i, 0)),
                  pl.BlockSpec((d, tn), lambda i, j: (0, j)),
                  pl.BlockSpec((d, tn), lambda i, j: (0, j))],
        out_specs=pl.BlockSpec((tm, tn), lambda i, j: (i, j)),
        out_shape=jax.ShapeDtypeStruct((m, n), BF16),
        compiler_params=_params(("arbitrary", "arbitrary")),
        name="ffn_up",
    )(h, wg, wu)


def _ffn_down_kernel(a_ref, wd_ref, x_ref, gpost_ref, gnext_ref, xo_ref, h_ref, *, tm):
    k = pl.program_id(1)
    part = jnp.dot(a_ref[...], wd_ref[...], preferred_element_type=F32)

    @pl.when(k == 0)
    def _first():
        xo_ref[...] = part

    @pl.when(k > 0)
    def _rest():
        xo_ref[...] += part

    @pl.when(k == pl.num_programs(1) - 1)
    def _tail():
        _residual_tail(xo_ref, x_ref, gpost_ref, gnext_ref, h_ref, tm)


def _ffn_down(a, wd, x, gpost, gnext, tm=512, tk=FF_TILE):
    m, d = x.shape
    kk = a.shape[1]
    row = lambda i, k: (i, 0)
    return pl.pallas_call(
        functools.partial(_ffn_down_kernel, tm=tm),
        grid=(m // tm, kk // tk),
        in_specs=[pl.BlockSpec((tm, tk), lambda i, k: (i, k)),
                  pl.BlockSpec((tk, d), lambda i, k: (k, 0)),
                  pl.BlockSpec((tm, d), row),
                  _resident(gpost.shape), _resident(gnext.shape)],
        out_specs=[pl.BlockSpec((tm, d), row), pl.BlockSpec((tm, d), row)],
        out_shape=[jax.ShapeDtypeStruct((m, d), F32), jax.ShapeDtypeStruct((m, d), BF16)],
        compiler_params=_params(("arbitrary", "arbitrary")),
        name="ffn_down",
    )(a, wd, x, gpost, gnext)


def kernel(x, mem, norm_mix_pre, norm_mix_post, w_in, gmlp_ws, gmlp_b, gmlp_ln_g, gmlp_ln_b,
           conf_w, conf_b, conf_ln_g, conf_ln_b, sc_w, w_out, norm_x_pre, norm_x_post, norm_mem,
           wq, wkv, wo, norm_ffn_pre, norm_ffn_post, w_gu, w_down):
    bsz, seq, d = x.shape
    assert bsz == 1, "causal-conv history is carried across row tiles of one sequence"
    layers = w_in.shape[0]
    d_ff = w_down.shape[1]
    ff_pad = -d_ff % FF_TILE
    xw = wq.shape[2]

    w_in_b = w_in.astype(BF16)
    w_out_b = w_out.astype(BF16)
    wq_b = wq.astype(BF16)
    wkv_b = wkv.astype(BF16)
    wo_b = wo.astype(BF16)
    wg_b = jnp.pad(w_gu[:, :, :d_ff].astype(BF16), ((0, 0), (0, 0), (0, ff_pad)))
    wu_b = jnp.pad(w_gu[:, :, d_ff:].astype(BF16), ((0, 0), (0, 0), (0, ff_pad)))
    wd_b = jnp.pad(w_down.astype(BF16), ((0, 0), (0, ff_pad), (0, 0)))

    row = lambda a: a.reshape(layers, 1, a.shape[-1])
    g_mix_pre, g_mix_post = row(norm_mix_pre), row(norm_mix_post)
    g_x_pre, g_x_post = row(norm_x_pre), row(norm_x_post)
    g_ffn_pre, g_ffn_post = row(norm_ffn_pre), row(norm_ffn_post)
    lng, lnb = row(gmlp_ln_g), row(gmlp_ln_b)
    cb, cng, cnb = row(conf_b), row(conf_ln_g), row(conf_ln_b)
    bsT = jnp.swapaxes(gmlp_b, 1, 2)

    kv = _memory_kv(mem[0], row(norm_mem), wkv_b)
    kT = jnp.swapaxes(kv[:, :, :xw], 1, 2)
    vv = kv[:, :, xw:]

    xs = x[0]
    h = _prenorm(xs, g_mix_pre[0])
    for l in range(layers):
        z = _proj(h, w_in_b[l], 1024, 1024, "mix_in")
        xs, h = _mixer(z, xs, w_out_b[l], gmlp_ws[l], bsT[l], lng[l], lnb[l], conf_w[l], cb[l],
                       cng[l], cnb[l], sc_w[l], g_mix_post[l], g_x_pre[l])
        q = _proj(h, wq_b[l], 1024, 1024, "q_proj")
        xs, h = _attention(q, kT[l], vv[l], wo_b[l], xs, g_x_post[l], g_ffn_pre[l])
        a = _ffn_up(h, wg_b[l], wu_b[l])
        g_next = g_mix_pre[l + 1] if l + 1 < layers else g_ffn_pre[l]
        xs, h = _ffn_down(a, wd_b[l], xs, g_ffn_post[l], g_next)
    return xs[None]
```

```python
import functools
import math

import jax
import jax.numpy as jnp
from jax import lax
from jax.experimental import pallas as pl
from jax.experimental.pallas import tpu as pltpu

EPS = 1e-6
CHUNK = 128
HEAD_DIM = 128
CONF_K = 31
SC_K = 3
X_HEADS = 4
SUBLANES = 8
LANES = 128
TAIL_LANES = 512
CONF_HALO = 32
SC_HALO = 8
ROWS = 32
CONV_LANES = 256
FF_TILE = 512
VMEM_LIMIT = 60 * 1024 * 1024

BF16 = jnp.bfloat16
F32 = jnp.float32


def _params(sem):
    return pltpu.CompilerParams(dimension_semantics=sem, vmem_limit_bytes=VMEM_LIMIT)


def _layer_param(arr, layer):
    tail = arr.shape[1:]
    idx = (layer,) + (0,) * len(tail)
    return pl.BlockSpec((None,) + tail, lambda *_: idx, pipeline_mode=pl.Buffered(1))


def _rms_scale(v):
    return lax.rsqrt(jnp.mean(v * v, axis=-1, keepdims=True) + EPS)


def _layer_norm(v, g, b):
    mu = jnp.mean(v, axis=-1, keepdims=True)
    vc = v - mu
    return vc * lax.rsqrt(jnp.mean(vc * vc, axis=-1, keepdims=True) + EPS) * g + b


def _sigmoid(v):
    return 1.0 / (1.0 + jnp.exp(-v))


def _gelu_tanh(v):
    c = math.sqrt(2.0 / math.pi)
    return 0.5 * v * (1.0 + jnp.tanh(c * (v + 0.044715 * (v * v * v))))


def _row_pieces(n_rows, body, unroll=1):
    def step(r, carry):
        body(pl.ds(pl.multiple_of(r * ROWS, ROWS), ROWS))
        return carry
    lax.fori_loop(0, n_rows // ROWS, step, 0, unroll=unroll)


def _prenorm_kernel(x_ref, g_ref, h_ref, *, tm):
    g = g_ref[...]

    def piece(rows):
        x = x_ref[rows, :]
        h_ref[rows, :] = (x * _rms_scale(x) * g).astype(h_ref.dtype)

    _row_pieces(tm, piece, unroll=2)


def _prenorm(x, g, layer, tm=512):
    m, d = x.shape
    return pl.pallas_call(
        functools.partial(_prenorm_kernel, tm=tm),
        grid=(m // tm,),
        in_specs=[pl.BlockSpec((tm, d), lambda i: (i, 0)), _layer_param(g, layer)],
        out_specs=pl.BlockSpec((tm, d), lambda i: (i, 0)),
        out_shape=jax.ShapeDtypeStruct((m, d), BF16),
        compiler_params=_params(("arbitrary",)),
        name="prenorm",
    )(x, g)


def _proj_kernel(a_ref, w_ref, o_ref):
    o_ref[...] = jnp.dot(a_ref[...], w_ref[...], preferred_element_type=F32).astype(o_ref.dtype)


def _proj(a, w, layer, tm, tn, name):
    m, k = a.shape
    n = w.shape[2]
    return pl.pallas_call(
        _proj_kernel,
        grid=(m // tm, n // tn),
        in_specs=[pl.BlockSpec((tm, k), lambda i, j: (i, 0)),
                  pl.BlockSpec((None, k, tn), lambda i, j: (layer, 0, j))],
        out_specs=pl.BlockSpec((tm, tn), lambda i, j: (i, j)),
        out_shape=jax.ShapeDtypeStruct((m, n), BF16),
        compiler_params=_params(("arbitrary", "arbitrary")),
        name=name,
    )(a, w)


def _lane_fold(part, v):
    for t in range(v.shape[1] // LANES):
        s = v[:, t * LANES:(t + 1) * LANES]
        part = s if part is None else part + s
    return part


def _rms_from_partial(part, width):
    return lax.rsqrt(jnp.sum(part, axis=-1, keepdims=True) * (1.0 / width) + EPS)


def _residual_tail(y_ref, x_ref, gpost_ref, gnext_ref, h_ref, n_rows):
    width = y_ref.shape[1]
    blocks = [slice(c, c + TAIL_LANES) for c in range(0, width, TAIL_LANES)]

    def piece(rows):
        part = None
        for cols in blocks:
            y = y_ref[rows, cols]
            part = _lane_fold(part, y * y)
        rs = _rms_from_partial(part, width)
        part = None
        for cols in blocks:
            xn = x_ref[rows, cols] + y_ref[rows, cols] * rs * gpost_ref[:, cols]
            y_ref[rows, cols] = xn
            if h_ref is not None:
                part = _lane_fold(part, xn * xn)
        if h_ref is not None:
            rs = _rms_from_partial(part, width)
            for cols in blocks:
                h_ref[rows, cols] = (y_ref[rows, cols] * rs * gnext_ref[:, cols]).astype(h_ref.dtype)

    _row_pieces(n_rows, piece, unroll=2)


def _tail_specs(x, gpost, gnext, layer, next_layer, tm, row):
    m, d = x.shape
    in_specs = [pl.BlockSpec((tm, d), row), _layer_param(gpost, layer)]
    operands = [x, gpost]
    out_specs = [pl.BlockSpec((tm, d), row)]
    out_shape = [jax.ShapeDtypeStruct((m, d), F32)]
    if gnext is not None:
        in_specs.append(_layer_param(gnext, next_layer))
        operands.append(gnext)
        out_specs.append(pl.BlockSpec((tm, d), row))
        out_shape.append(jax.ShapeDtypeStruct((m, d), BF16))
    return in_specs, operands, out_specs, out_shape


def _causal_conv_piece(src_sc, w_ref, n_taps, halo, base, lanes, init):
    win = src_sc[pl.ds(base, halo + ROWS), lanes]
    acc = init
    for r in range(min(SUBLANES, n_taps)):
        shifted = win if r == 0 else pltpu.roll(win, r, axis=0)
        for q in range((n_taps - 1 - r) // SUBLANES + 1):
            k = n_taps - 1 - (SUBLANES * q + r)
            lo = halo - SUBLANES * q
            term = w_ref[k:k + 1, lanes] * shifted[lo:lo + ROWS, :]
            acc = term if acc is None else acc + term
    return acc


def _mixer_kernel(z_ref, wout_ref, ws_ref, bsT_ref, lng_ref, lnb_ref,
                  cw_ref, cb_ref, cng_ref, cnb_ref, scw_ref, x_ref, gpost_ref, gnext_ref,
                  xo_ref, h_ref,
                  wm_sc, v_sc, hb_sc, yb_sc, hc_sc, cat_sc, *, tm, wa, wb, wc):
    i = pl.program_id(0)
    heads = wa // HEAD_DIM
    zb0 = 2 * wa
    zc0 = 2 * wa + 2 * wb

    @pl.when(i == 0)
    def _init():
        hb_sc[0:CONF_HALO, :] = jnp.zeros((CONF_HALO, wb), F32)
        hc_sc[0:SC_HALO, :] = jnp.zeros((SC_HALO, wc), F32)
        row = lax.broadcasted_iota(jnp.int32, (CHUNK, CHUNK), 0)
        col = lax.broadcasted_iota(jnp.int32, (CHUNK, CHUNK), 1)
        for hh in range(heads):
            wm_sc[hh] = jnp.where(col <= row, ws_ref[hh], 0.0).astype(BF16)

    lng = lng_ref[...]
    lnb = lnb_ref[...]

    def norm_v(rows):
        v = _gelu_tanh(z_ref[rows, wa:2 * wa].astype(F32))
        v_sc[rows, :] = _layer_norm(v, lng, lnb).astype(BF16)

    _row_pieces(tm, norm_v, unroll=2)
    for c in range(tm // CHUNK):
        rows = slice(c * CHUNK, (c + 1) * CHUNK)
        for hh in range(heads):
            cols = slice(hh * HEAD_DIM, (hh + 1) * HEAD_DIM)
            mixed = jnp.dot(wm_sc[hh], v_sc[rows, cols], preferred_element_type=F32)
            mixed = mixed + bsT_ref[:, hh:hh + 1]
            u = _gelu_tanh(z_ref[rows, cols].astype(F32))
            cat_sc[rows, cols] = (u * mixed).astype(BF16)

    def glu(rows):
        a = z_ref[rows, zb0:zb0 + wb].astype(F32)
        g = z_ref[rows, zb0 + wb:zb0 + 2 * wb].astype(F32)
        dst = pl.ds(pl.multiple_of(rows.start + CONF_HALO, SUBLANES), ROWS)
        hb_sc[dst, :] = a * _sigmoid(g)

    _row_pieces(tm, glu, unroll=2)

    def conv_b(rows):
        for lb in range(wb // CONV_LANES):
            lanes = slice(lb * CONV_LANES, (lb + 1) * CONV_LANES)
            init = jnp.broadcast_to(cb_ref[:, lanes], (ROWS, CONV_LANES))
            yb_sc[rows, lanes] = _causal_conv_piece(
                hb_sc, cw_ref, CONF_K, CONF_HALO, rows.start, lanes, init)

    _row_pieces(tm, conv_b)
    hb_sc[0:CONF_HALO, :] = hb_sc[tm:tm + CONF_HALO, :]
    cng = cng_ref[...]
    cnb = cnb_ref[...]

    def norm_b(rows):
        yb = _layer_norm(yb_sc[rows, :], cng, cnb)
        cat_sc[rows, wa:wa + wb] = (yb * _sigmoid(yb)).astype(BF16)

    _row_pieces(tm, norm_b, unroll=2)

    def gate_c(rows):
        cc = z_ref[rows, zc0 + wc:zc0 + 2 * wc].astype(F32)
        hh_ = z_ref[rows, zc0 + 2 * wc:zc0 + 3 * wc].astype(F32)
        dst = pl.ds(pl.multiple_of(rows.start + SC_HALO, SUBLANES), ROWS)
        hc_sc[dst, :] = cc * hh_

    _row_pieces(tm, gate_c, unroll=2)

    def conv_c(rows):
        for lb in range(wc // CONV_LANES):
            lanes = slice(lb * CONV_LANES, (lb + 1) * CONV_LANES)
            conv = _causal_conv_piece(hc_sc, scw_ref, SC_K, SC_HALO, rows.start, lanes, None)
            bgate = z_ref[rows, zc0 + lb * CONV_LANES:zc0 + (lb + 1) * CONV_LANES].astype(F32)
            cat_sc[rows, wa + wb + lb * CONV_LANES:wa + wb + (lb + 1) * CONV_LANES] = (
                bgate * conv).astype(BF16)

    _row_pieces(tm, conv_c)
    hc_sc[0:SC_HALO, :] = hc_sc[tm:tm + SC_HALO, :]

    xo_ref[...] = jnp.dot(cat_sc[...], wout_ref[...], preferred_element_type=F32)
    _residual_tail(xo_ref, x_ref, gpost_ref, gnext_ref, h_ref, tm)


def _mixer(z, x, w_out, ws, bsT, lng, lnb, cw, cb, cng, cnb, scw, gpost, gnext, layer, tm=128):
    m, d = x.shape
    d_in = z.shape[1]
    wa = ws.shape[1] * HEAD_DIM
    wb = cw.shape[2]
    wc = scw.shape[2]
    assert d_in == 2 * wa + 2 * wb + 3 * wc and w_out.shape[1] == wa + wb + wc
    assert cw.shape[1] == CONF_K and scw.shape[1] == SC_K and tm % CHUNK == 0
    row = lambda i: (i, 0)
    weights = [w_out, ws, bsT, lng, lnb, cw, cb, cng, cnb, scw]
    t_in, t_ops, out_specs, out_shape = _tail_specs(x, gpost, gnext, layer, layer, tm, row)
    return pl.pallas_call(
        functools.partial(_mixer_kernel, tm=tm, wa=wa, wb=wb, wc=wc),
        grid=(m // tm,),
        in_specs=[pl.BlockSpec((tm, d_in), row)] + [_layer_param(a, layer) for a in weights] + t_in,
        out_specs=out_specs,
        out_shape=out_shape,
        scratch_shapes=[pltpu.VMEM(ws.shape[1:], BF16),
                        pltpu.VMEM((tm, wa), BF16),
                        pltpu.VMEM((CONF_HALO + tm, wb), F32),
                        pltpu.VMEM((tm, wb), F32),
                        pltpu.VMEM((SC_HALO + tm, wc), F32),
                        pltpu.VMEM((tm, wa + wb + wc), BF16)],
        compiler_params=_params(("arbitrary",)),
        name="mixer",
    )(z, *weights, *t_ops)


def _kv_kernel(mem_ref, g_ref, w_ref, o_ref):
    mem = mem_ref[...]
    mn = (mem * _rms_scale(mem) * g_ref[...]).astype(BF16)
    o_ref[...] = jnp.dot(mn, w_ref[...], preferred_element_type=F32).astype(o_ref.dtype)


def _memory_kv(mem, g_mem, wkv, tn=1024):
    n_mem, d = mem.shape
    layers, _, n = wkv.shape
    return pl.pallas_call(
        _kv_kernel,
        grid=(layers, n // tn),
        in_specs=[pl.BlockSpec((n_mem, d), lambda l, j: (0, 0)),
                  pl.BlockSpec((None, 1, d), lambda l, j: (l, 0, 0)),
                  pl.BlockSpec((None, d, tn), lambda l, j: (l, 0, j))],
        out_specs=pl.BlockSpec((None, n_mem, tn), lambda l, j: (l, 0, j)),
        out_shape=jax.ShapeDtypeStruct((layers, n_mem, n), BF16),
        compiler_params=_params(("arbitrary", "arbitrary")),
        name="memory_kv",
    )(mem, g_mem, wkv)


def _attn_kernel(q_ref, kT_ref, v_ref, wo_ref, x_ref, gpost_ref, gnext_ref,
                 xo_ref, h_ref, o_sc, *, tm, dh):
    scale = dh ** -0.5
    for hd in range(X_HEADS):
        cols = slice(hd * dh, (hd + 1) * dh)
        s = jnp.dot(q_ref[:, cols], kT_ref[cols, :], preferred_element_type=F32) * scale
        p = jnp.exp(s - jnp.max(s, axis=-1, keepdims=True))
        p = p / jnp.sum(p, axis=-1, keepdims=True)
        o = jnp.dot(p.astype(BF16), v_ref[:, cols], preferred_element_type=F32)
        o_sc[:, cols] = o.astype(BF16)
    xo_ref[...] = jnp.dot(o_sc[...], wo_ref[...], preferred_element_type=F32)
    _residual_tail(xo_ref, x_ref, gpost_ref, gnext_ref, h_ref, tm)


def _attention(q, kT, v, wo, x, gpost, gnext, layer, tm=256):
    m, d = x.shape
    xw = q.shape[1]
    row = lambda i: (i, 0)
    weights = [kT, v, wo]
    t_in, t_ops, out_specs, out_shape = _tail_specs(x, gpost, gnext, layer, layer, tm, row)
    return pl.pallas_call(
        functools.partial(_attn_kernel, tm=tm, dh=xw // X_HEADS),
        grid=(m // tm,),
        in_specs=[pl.BlockSpec((tm, xw), row)] + [_layer_param(a, layer) for a in weights] + t_in,
        out_specs=out_specs,
        out_shape=out_shape,
        scratch_shapes=[pltpu.VMEM((tm, xw), BF16)],
        compiler_params=_params(("arbitrary",)),
        name="attention",
    )(q, *weights, *t_ops)


def _ffn_up_kernel(h_ref, wg_ref, wu_ref, a_ref):
    h = h_ref[...]
    gate = jnp.dot(h, wg_ref[...], preferred_element_type=F32)
    up = jnp.dot(h, wu_ref[...], preferred_element_type=F32)
    a_ref[...] = (gate * _sigmoid(gate) * up).astype(a_ref.dtype)


def _ffn_up(h, wgu, layer, tm=1024, tn=FF_TILE):
    m, d = h.shape
    n_half = wgu.shape[2] // 2
    up0 = n_half // tn
    return pl.pallas_call(
        _ffn_up_kernel,
        grid=(m // tm, n_half // tn),
        in_specs=[pl.BlockSpec((tm, d), lambda i, j: (i, 0)),
                  pl.BlockSpec((None, d, tn), lambda i, j: (layer, 0, j)),
                  pl.BlockSpec((None, d, tn), lambda i, j: (layer, 0, j + up0))],
        out_specs=pl.BlockSpec((tm, tn), lambda i, j: (i, j)),
        out_shape=jax.ShapeDtypeStruct((m, n_half), BF16),
        compiler_params=_params(("arbitrary", "arbitrary")),
        name="ffn_up",
    )(h, wgu, wgu)


def _ffn_down_kernel(a_ref, wd_ref, x_ref, gpost_ref, *rest, tm, emit_h):
    if emit_h:
        gnext_ref, xo_ref, h_ref = rest
    else:
        (xo_ref,), gnext_ref, h_ref = rest, None, None
    k = pl.program_id(1)

    @pl.when(k == 0)
    def _first():
        xo_ref[...] = jnp.dot(a_ref[...], wd_ref[...], preferred_element_type=F32)

    @pl.when(k > 0)
    def _rest():
        xo_ref[...] += jnp.dot(a_ref[...], wd_ref[...], preferred_element_type=F32)

    @pl.when(k == pl.num_programs(1) - 1)
    def _tail():
        _residual_tail(xo_ref, x_ref, gpost_ref, gnext_ref, h_ref, tm)


def _ffn_down(a, wd, x, gpost, gnext, layer, next_layer, tm=512, tk=FF_TILE):
    m, d = x.shape
    kk = a.shape[1]
    row = lambda i, k: (i, 0)
    t_in, t_ops, out_specs, out_shape = _tail_specs(x, gpost, gnext, layer, next_layer, tm, row)
    return pl.pallas_call(
        functools.partial(_ffn_down_kernel, tm=tm, emit_h=gnext is not None),
        grid=(m // tm, kk // tk),
        in_specs=[pl.BlockSpec((tm, tk), lambda i, k: (i, k)),
                  pl.BlockSpec((None, tk, d), lambda i, k: (layer, k, 0))] + t_in,
        out_specs=out_specs,
        out_shape=out_shape,
        compiler_params=_params(("arbitrary", "arbitrary")),
        name="ffn_down",
    )(a, wd, *t_ops)


def kernel(x, mem, norm_mix_pre, norm_mix_post, w_in, gmlp_ws, gmlp_b, gmlp_ln_g, gmlp_ln_b,
           conf_w, conf_b, conf_ln_g, conf_ln_b, sc_w, w_out, norm_x_pre, norm_x_post, norm_mem,
           wq, wkv, wo, norm_ffn_pre, norm_ffn_post, w_gu, w_down):
    bsz, seq, d = x.shape
    assert bsz == 1, "causal-conv history is carried across row tiles of one sequence"
    layers = w_in.shape[0]
    d_ff = w_down.shape[1]
    ff_pad = -d_ff % FF_TILE
    xw = wq.shape[2]

    w_in_b = w_in.astype(BF16)
    w_out_b = w_out.astype(BF16)
    wq_b = wq.astype(BF16)
    wkv_b = wkv.astype(BF16)
    wo_b = wo.astype(BF16)
    col_pad = ((0, 0), (0, 0), (0, ff_pad))
    wgu_b = jnp.concatenate([jnp.pad(w_gu[:, :, :d_ff], col_pad),
                             jnp.pad(w_gu[:, :, d_ff:], col_pad)], axis=-1).astype(BF16)
    wd_b = jnp.pad(w_down, ((0, 0), (0, ff_pad), (0, 0))).astype(BF16)

    row = lambda a: a.reshape(layers, 1, a.shape[-1])
    g_mix_pre, g_mix_post = row(norm_mix_pre), row(norm_mix_post)
    g_x_pre, g_x_post = row(norm_x_pre), row(norm_x_post)
    g_ffn_pre, g_ffn_post = row(norm_ffn_pre), row(norm_ffn_post)
    lng, lnb = row(gmlp_ln_g), row(gmlp_ln_b)
    cb, cng, cnb = row(conf_b), row(conf_ln_g), row(conf_ln_b)
    bsT = jnp.swapaxes(gmlp_b, 1, 2)

    kv = _memory_kv(mem[0], row(norm_mem), wkv_b)
    kT = jnp.swapaxes(kv[:, :, :xw], 1, 2)
    vv = kv[:, :, xw:]

    xs = x[0]
    h = _prenorm(xs, g_mix_pre, 0)
    for l in range(layers):
        z = _proj(h, w_in_b, l, 1024, 1024, "mix_in")
        xs, h = _mixer(z, xs, w_out_b, gmlp_ws, bsT, lng, lnb, conf_w, cb, cng, cnb, sc_w,
                       g_mix_post, g_x_pre, l)
        q = _proj(h, wq_b, l, 1024, 1024, "q_proj")
        xs, h = _attention(q, kT, vv, wo_b, xs, g_x_post, g_ffn_pre, l)
        a = _ffn_up(h, wgu_b, l)
        if l + 1 < layers:
            xs, h = _ffn_down(a, wd_b, xs, g_ffn_post, g_mix_pre, l, l + 1)
        else:
            (xs,) = _ffn_down(a, wd_b, xs, g_ffn_post, None, l, l)
    return xs[None]
```

```python
import functools
import math

import jax
import jax.numpy as jnp
from jax import lax
from jax.experimental import pallas as pl
from jax.experimental.pallas import tpu as pltpu

EPS = 1e-6
CHUNK = 128
HEAD_DIM = 128
CONF_K = 31
SC_K = 3
X_HEADS = 4
SUBLANES = 8
LANES = 128
TAIL_LANES = 512
CONF_HALO = 32
SC_HALO = 8
ROWS = 32
CONV_LANES = 256
FF_TILE = 512
VMEM_LIMIT = 60 * 1024 * 1024

BF16 = jnp.bfloat16
F32 = jnp.float32


def _params(sem):
    return pltpu.CompilerParams(dimension_semantics=sem, vmem_limit_bytes=VMEM_LIMIT)


def _layer_param(arr, layer):
    tail = arr.shape[1:]
    idx = (layer,) + (0,) * len(tail)
    return pl.BlockSpec((None,) + tail, lambda *_: idx, pipeline_mode=pl.Buffered(1))


def _rms_scale(v):
    return lax.rsqrt(jnp.mean(v * v, axis=-1, keepdims=True) + EPS)


def _layer_norm(v, g, b):
    mu = jnp.mean(v, axis=-1, keepdims=True)
    vc = v - mu
    return vc * lax.rsqrt(jnp.mean(vc * vc, axis=-1, keepdims=True) + EPS) * g + b


def _sigmoid(v):
    return 1.0 / (1.0 + jnp.exp(-v))


def _gelu_tanh(v):
    c = math.sqrt(2.0 / math.pi)
    return 0.5 * v * (1.0 + jnp.tanh(c * (v + 0.044715 * (v * v * v))))


def _row_pieces(n_rows, body, unroll=1, static=False):
    if static:
        for r in range(n_rows // ROWS):
            body(pl.ds(r * ROWS, ROWS))
        return

    def step(r, carry):
        body(pl.ds(pl.multiple_of(r * ROWS, ROWS), ROWS))
        return carry
    lax.fori_loop(0, n_rows // ROWS, step, 0, unroll=unroll)


def _prenorm_kernel(x_ref, g_ref, h_ref, *, tm):
    g = g_ref[...]

    def piece(rows):
        x = x_ref[rows, :]
        h_ref[rows, :] = (x * _rms_scale(x) * g).astype(h_ref.dtype)

    _row_pieces(tm, piece, unroll=2)


def _prenorm(x, g, layer, tm=512):
    m, d = x.shape
    return pl.pallas_call(
        functools.partial(_prenorm_kernel, tm=tm),
        grid=(m // tm,),
        in_specs=[pl.BlockSpec((tm, d), lambda i: (i, 0)), _layer_param(g, layer)],
        out_specs=pl.BlockSpec((tm, d), lambda i: (i, 0)),
        out_shape=jax.ShapeDtypeStruct((m, d), BF16),
        compiler_params=_params(("arbitrary",)),
        name="prenorm",
    )(x, g)


def _proj_kernel(a_ref, w_ref, o_ref):
    o_ref[...] = jnp.dot(a_ref[...], w_ref[...], preferred_element_type=F32).astype(o_ref.dtype)


def _proj(a, w, layer, tm, tn, name):
    m, k = a.shape
    n = w.shape[2]
    return pl.pallas_call(
        _proj_kernel,
        grid=(m // tm, n // tn),
        in_specs=[pl.BlockSpec((tm, k), lambda i, j: (i, 0)),
                  pl.BlockSpec((None, k, tn), lambda i, j: (layer, 0, j))],
        out_specs=pl.BlockSpec((tm, tn), lambda i, j: (i, j)),
        out_shape=jax.ShapeDtypeStruct((m, n), BF16),
        compiler_params=_params(("arbitrary", "arbitrary")),
        name=name,
    )(a, w)


def _lane_fold(part, v):
    for t in range(v.shape[1] // LANES):
        s = v[:, t * LANES:(t + 1) * LANES]
        part = s if part is None else part + s
    return part


def _rms_from_partial(part, width):
    return lax.rsqrt(jnp.sum(part, axis=-1, keepdims=True) * (1.0 / width) + EPS)


def _residual_tail(y_ref, x_ref, gpost_ref, gnext_ref, xo_ref, h_ref, n_rows, static=False):
    width = y_ref.shape[1]
    blocks = [slice(c, c + TAIL_LANES) for c in range(0, width, TAIL_LANES)]

    def piece(rows):
        part = None
        for cols in blocks:
            y = y_ref[rows, cols]
            part = _lane_fold(part, y * y)
        rs = _rms_from_partial(part, width)
        part = None
        for cols in blocks:
            xn = x_ref[rows, cols] + y_ref[rows, cols] * rs * gpost_ref[:, cols]
            xo_ref[rows, cols] = xn
            if h_ref is not None:
                part = _lane_fold(part, xn * xn)
        if h_ref is not None:
            rs = _rms_from_partial(part, width)
            for cols in blocks:
                h_ref[rows, cols] = (xo_ref[rows, cols] * rs * gnext_ref[:, cols]).astype(h_ref.dtype)

    _row_pieces(n_rows, piece, unroll=2, static=static)


def _tail_specs(x, gpost, gnext, layer, next_layer, tm, row):
    m, d = x.shape
    in_specs = [pl.BlockSpec((tm, d), row), _layer_param(gpost, layer)]
    operands = [x, gpost]
    out_specs = [pl.BlockSpec((tm, d), row)]
    out_shape = [jax.ShapeDtypeStruct((m, d), F32)]
    if gnext is not None:
        in_specs.append(_layer_param(gnext, next_layer))
        operands.append(gnext)
        out_specs.append(pl.BlockSpec((tm, d), row))
        out_shape.append(jax.ShapeDtypeStruct((m, d), BF16))
    return in_specs, operands, out_specs, out_shape


def _causal_conv_piece(src_sc, w_ref, n_taps, halo, base, lanes, init):
    win = src_sc[pl.ds(base, halo + ROWS), lanes]
    acc = init
    for r in range(min(SUBLANES, n_taps)):
        shifted = win if r == 0 else pltpu.roll(win, r, axis=0)
        for q in range((n_taps - 1 - r) // SUBLANES + 1):
            k = n_taps - 1 - (SUBLANES * q + r)
            lo = halo - SUBLANES * q
            term = w_ref[k:k + 1, lanes] * shifted[lo:lo + ROWS, :]
            acc = term if acc is None else acc + term
    return acc


def _mixer_kernel(z_ref, wout_ref, ws_ref, bsT_ref, lng_ref, lnb_ref,
                  cw_ref, cb_ref, cng_ref, cnb_ref, scw_ref, x_ref, gpost_ref, gnext_ref,
                  xo_ref, h_ref,
                  wm_sc, v_sc, hb_sc, yb_sc, hc_sc, cat_new, cat_cur, y_new, y_cur,
                  *, tm, wa, wb, wc):
    i = pl.program_id(0)
    heads = wa // HEAD_DIM
    zb0 = 2 * wa
    zc0 = 2 * wa + 2 * wb

    @pl.when(i == 0)
    def _init():
        hb_sc[0:CONF_HALO, :] = jnp.zeros((CONF_HALO, wb), F32)
        hc_sc[0:SC_HALO, :] = jnp.zeros((SC_HALO, wc), F32)
        cat_new[...] = jnp.zeros(cat_new.shape, BF16)
        y_new[...] = jnp.zeros(y_new.shape, F32)
        row = lax.broadcasted_iota(jnp.int32, (CHUNK, CHUNK), 0)
        col = lax.broadcasted_iota(jnp.int32, (CHUNK, CHUNK), 1)
        for hh in range(heads):
            wm_sc[hh] = jnp.where(col <= row, ws_ref[hh], 0.0).astype(BF16)

    cat_cur[...] = cat_new[...]
    y_cur[...] = y_new[...]

    _residual_tail(y_cur, x_ref, gpost_ref, gnext_ref, xo_ref, h_ref, tm, static=True)

    y_new[...] = jnp.dot(cat_cur[...], wout_ref[...], preferred_element_type=F32)

    lng = lng_ref[...]
    lnb = lnb_ref[...]

    def norm_v(rows):
        v = _gelu_tanh(z_ref[rows, wa:2 * wa].astype(F32))
        v_sc[rows, :] = _layer_norm(v, lng, lnb).astype(BF16)

    _row_pieces(tm, norm_v, static=True)
    for c in range(tm // CHUNK):
        rows = slice(c * CHUNK, (c + 1) * CHUNK)
        for hh in range(heads):
            cols = slice(hh * HEAD_DIM, (hh + 1) * HEAD_DIM)
            mixed = jnp.dot(wm_sc[hh], v_sc[rows, cols], preferred_element_type=F32)
            mixed = mixed + bsT_ref[:, hh:hh + 1]
            u = _gelu_tanh(z_ref[rows, cols].astype(F32))
            cat_new[rows, cols] = (u * mixed).astype(BF16)

    def glu(rows):
        a = z_ref[rows, zb0:zb0 + wb].astype(F32)
        g = z_ref[rows, zb0 + wb:zb0 + 2 * wb].astype(F32)
        hb_sc[pl.ds(rows.start + CONF_HALO, ROWS), :] = a * _sigmoid(g)

    _row_pieces(tm, glu, static=True)

    def conv_b(rows):
        for lb in range(wb // CONV_LANES):
            lanes = slice(lb * CONV_LANES, (lb + 1) * CONV_LANES)
            init = jnp.broadcast_to(cb_ref[:, lanes], (ROWS, CONV_LANES))
            yb_sc[rows, lanes] = _causal_conv_piece(
                hb_sc, cw_ref, CONF_K, CONF_HALO, rows.start, lanes, init)

    _row_pieces(tm, conv_b, static=True)
    hb_sc[0:CONF_HALO, :] = hb_sc[tm:tm + CONF_HALO, :]
    cng = cng_ref[...]
    cnb = cnb_ref[...]

    def norm_b(rows):
        yb = _layer_norm(yb_sc[rows, :], cng, cnb)
        cat_new[rows, wa:wa + wb] = (yb * _sigmoid(yb)).astype(BF16)

    _row_pieces(tm, norm_b, static=True)

    def gate_c(rows):
        cc = z_ref[rows, zc0 + wc:zc0 + 2 * wc].astype(F32)
        hh_ = z_ref[rows, zc0 + 2 * wc:zc0 + 3 * wc].astype(F32)
        hc_sc[pl.ds(rows.start + SC_HALO, ROWS), :] = cc * hh_

    _row_pieces(tm, gate_c, static=True)

    def conv_c(rows):
        for lb in range(wc // CONV_LANES):
            lanes = slice(lb * CONV_LANES, (lb + 1) * CONV_LANES)
            conv = _causal_conv_piece(hc_sc, scw_ref, SC_K, SC_HALO, rows.start, lanes, None)
            bgate = z_ref[rows, zc0 + lb * CONV_LANES:zc0 + (lb + 1) * CONV_LANES].astype(F32)
            cat_new[rows, wa + wb + lb * CONV_LANES:wa + wb + (lb + 1) * CONV_LANES] = (
                bgate * conv).astype(BF16)

    _row_pieces(tm, conv_c, static=True)
    hc_sc[0:SC_HALO, :] = hc_sc[tm:tm + SC_HALO, :]


def _mixer(z, x, w_out, ws, bsT, lng, lnb, cw, cb, cng, cnb, scw, gpost, gnext, layer, tm=128):
    m, d = x.shape
    d_in = z.shape[1]
    wa = ws.shape[1] * HEAD_DIM
    wb = cw.shape[2]
    wc = scw.shape[2]
    assert d_in == 2 * wa + 2 * wb + 3 * wc and w_out.shape[1] == wa + wb + wc
    assert cw.shape[1] == CONF_K and scw.shape[1] == SC_K and tm % CHUNK == 0
    n_tiles = m // tm
    mix_row = lambda i: (jnp.minimum(i, n_tiles - 1), 0)
    tail_row = lambda i: (jnp.clip(i - 2, 0, n_tiles - 1), 0)
    weights = [w_out, ws, bsT, lng, lnb, cw, cb, cng, cnb, scw]
    t_in, t_ops, out_specs, out_shape = _tail_specs(x, gpost, gnext, layer, layer, tm, tail_row)
    return pl.pallas_call(
        functools.partial(_mixer_kernel, tm=tm, wa=wa, wb=wb, wc=wc),
        grid=(n_tiles + 2,),
        in_specs=[pl.BlockSpec((tm, d_in), mix_row)] + [_layer_param(a, layer) for a in weights] + t_in,
        out_specs=out_specs,
        out_shape=out_shape,
        scratch_shapes=[pltpu.VMEM(ws.shape[1:], BF16),
                        pltpu.VMEM((tm, wa), BF16),
                        pltpu.VMEM((CONF_HALO + tm, wb), F32),
                        pltpu.VMEM((tm, wb), F32),
                        pltpu.VMEM((SC_HALO + tm, wc), F32),
                        pltpu.VMEM((tm, wa + wb + wc), BF16),
                        pltpu.VMEM((tm, wa + wb + wc), BF16),
                        pltpu.VMEM((tm, d), F32),
                        pltpu.VMEM((tm, d), F32)],
        compiler_params=_params(("arbitrary",)),
        name="mixer",
    )(z, *weights, *t_ops)


def _kv_kernel(mem_ref, g_ref, w_ref, o_ref):
    mem = mem_ref[...]
    mn = (mem * _rms_scale(mem) * g_ref[...]).astype(BF16)
    o_ref[...] = jnp.dot(mn, w_ref[...], preferred_element_type=F32).astype(o_ref.dtype)


def _memory_kv(mem, g_mem, wkv, tn=1024):
    n_mem, d = mem.shape
    layers, _, n = wkv.shape
    return pl.pallas_call(
        _kv_kernel,
        grid=(layers, n // tn),
        in_specs=[pl.BlockSpec((n_mem, d), lambda l, j: (0, 0)),
                  pl.BlockSpec((None, 1, d), lambda l, j: (l, 0, 0)),
                  pl.BlockSpec((None, d, tn), lambda l, j: (l, 0, j))],
        out_specs=pl.BlockSpec((None, n_mem, tn), lambda l, j: (l, 0, j)),
        out_shape=jax.ShapeDtypeStruct((layers, n_mem, n), BF16),
        compiler_params=_params(("arbitrary", "arbitrary")),
        name="memory_kv",
    )(mem, g_mem, wkv)


def _attn_kernel(q_ref, kT_ref, v_ref, wo_ref, x_ref, gpost_ref, gnext_ref,
                 xo_ref, h_ref, o_sc, y_new, y_cur, *, tm, dh):
    i = pl.program_id(0)

    @pl.when(i == 0)
    def _init():
        y_new[...] = jnp.zeros(y_new.shape, F32)

    y_cur[...] = y_new[...]
    _residual_tail(y_cur, x_ref, gpost_ref, gnext_ref, xo_ref, h_ref, tm, static=True)

    scale = dh ** -0.5
    for hd in range(X_HEADS):
        cols = slice(hd * dh, (hd + 1) * dh)
        s = jnp.dot(q_ref[:, cols], kT_ref[cols, :], preferred_element_type=F32) * scale
        p = jnp.exp(s - jnp.max(s, axis=-1, keepdims=True))
        p = p / jnp.sum(p, axis=-1, keepdims=True)
        o = jnp.dot(p.astype(BF16), v_ref[:, cols], preferred_element_type=F32)
        o_sc[:, cols] = o.astype(BF16)
    y_new[...] = jnp.dot(o_sc[...], wo_ref[...], preferred_element_type=F32)


def _attention(q, kT, v, wo, x, gpost, gnext, layer, tm=256):
    m, d = x.shape
    xw = q.shape[1]
    n_tiles = m // tm
    q_row = lambda i: (jnp.minimum(i, n_tiles - 1), 0)
    tail_row = lambda i: (jnp.maximum(i - 1, 0), 0)
    weights = [kT, v, wo]
    t_in, t_ops, out_specs, out_shape = _tail_specs(x, gpost, gnext, layer, layer, tm, tail_row)
    return pl.pallas_call(
        functools.partial(_attn_kernel, tm=tm, dh=xw // X_HEADS),
        grid=(n_tiles + 1,),
        in_specs=[pl.BlockSpec((tm, xw), q_row)] + [_layer_param(a, layer) for a in weights] + t_in,
        out_specs=out_specs,
        out_shape=out_shape,
        scratch_shapes=[pltpu.VMEM((tm, xw), BF16),
                        pltpu.VMEM((tm, d), F32),
                        pltpu.VMEM((tm, d), F32)],
        compiler_params=_params(("arbitrary",)),
        name="attention",
    )(q, *weights, *t_ops)


def _ffn_up_kernel(h_ref, wg_ref, wu_ref, a_ref):
    h = h_ref[...]
    gate = jnp.dot(h, wg_ref[...], preferred_element_type=F32)
    up = jnp.dot(h, wu_ref[...], preferred_element_type=F32)
    a_ref[...] = (gate * _sigmoid(gate) * up).astype(a_ref.dtype)


def _ffn_up(h, wg, wu, layer, tm=1024, tn=FF_TILE):
    m, d = h.shape
    n = wg.shape[2]
    return pl.pallas_call(
        _ffn_up_kernel,
        grid=(m // tm, n // tn),
        in_specs=[pl.BlockSpec((tm, d), lambda i, j: (i, 0)),
                  pl.BlockSpec((None, d, tn), lambda i, j: (layer, 0, j)),
                  pl.BlockSpec((None, d, tn), lambda i, j: (layer, 0, j))],
        out_specs=pl.BlockSpec((tm, tn), lambda i, j: (i, j)),
        out_shape=jax.ShapeDtypeStruct((m, n), BF16),
        compiler_params=_params(("arbitrary", "arbitrary")),
        name="ffn_up",
    )(h, wg, wu)


def _ffn_down_kernel(a_ref, wd_ref, x_ref, gpost_ref, *rest, n_tiles, emit_h):
    if emit_h:
        gnext_ref, xo_ref, h_ref, acc, y_stash = rest
    else:
        (xo_ref, acc, y_stash), gnext_ref, h_ref = rest, None, None
    i = pl.program_id(0)
    k = pl.program_id(1)
    n_chunks, chunk_rows, _ = y_stash.shape
    live = i < n_tiles
    spread = jnp.logical_and(k > 0, k < n_chunks)

    def stash():
        for c in range(n_chunks):
            y_stash[c] = acc[c * chunk_rows:(c + 1) * chunk_rows, :]

    def tail(c):
        _residual_tail(y_stash.at[c], x_ref, gpost_ref, gnext_ref, xo_ref, h_ref,
                       chunk_rows, static=True)

    def partial_sum():
        return jnp.dot(a_ref[...], wd_ref[...], preferred_element_type=F32)

    has_prev = i > 0

    @pl.when(jnp.logical_and(i == 0, k == 0))
    def _first_tile():
        acc[...] = partial_sum()

    @pl.when(jnp.logical_and(jnp.logical_and(live, has_prev), k == 0))
    def _first():
        stash()
        acc[...] = partial_sum()
        tail(0)

    @pl.when(jnp.logical_and(jnp.logical_and(live, has_prev), spread))
    def _early():
        acc[...] += partial_sum()
        tail(k)

    @pl.when(jnp.logical_and(jnp.logical_and(live, k > 0),
                             jnp.logical_or(i == 0, k >= n_chunks)))
    def _late():
        acc[...] += partial_sum()

    @pl.when(jnp.logical_and(jnp.logical_not(live), k == 0))
    def _drain_first():
        stash()
        tail(0)

    @pl.when(jnp.logical_and(jnp.logical_not(live), spread))
    def _drain():
        tail(k)


def _ffn_down(a, wd, x, gpost, gnext, layer, next_layer, tm=1024, tk=FF_TILE, n_chunks=8):
    m, d = x.shape
    kk = a.shape[1]
    tm = min(tm, m)
    n_tiles = m // tm
    k_steps = kk // tk
    n_chunks = min(n_chunks, k_steps)
    chunk_rows = tm // n_chunks
    assert chunk_rows % ROWS == 0
    k_blk = lambda i, k: jnp.where(i < n_tiles, k, k_steps - 1)
    chunk = lambda i, k: (jnp.where(i > 0, (i - 1) * n_chunks + jnp.minimum(k, n_chunks - 1), 0), 0)
    t_in, t_ops, out_specs, out_shape = _tail_specs(x, gpost, gnext, layer, next_layer,
                                                    chunk_rows, chunk)
    return pl.pallas_call(
        functools.partial(_ffn_down_kernel, n_tiles=n_tiles, emit_h=gnext is not None),
        grid=(n_tiles + 1, k_steps),
        in_specs=[pl.BlockSpec((tm, tk), lambda i, k: (jnp.minimum(i, n_tiles - 1), k_blk(i, k))),
                  pl.BlockSpec((None, tk, d), lambda i, k: (layer, k_blk(i, k), 0))] + t_in,
        out_specs=out_specs,
        out_shape=out_shape,
        scratch_shapes=[pltpu.VMEM((tm, d), F32),
                        pltpu.VMEM((n_chunks, chunk_rows, d), F32)],
        compiler_params=_params(("arbitrary", "arbitrary")),
        name="ffn_down",
    )(a, wd, *t_ops)


def kernel(x, mem, norm_mix_pre, norm_mix_post, w_in, gmlp_ws, gmlp_b, gmlp_ln_g, gmlp_ln_b,
           conf_w, conf_b, conf_ln_g, conf_ln_b, sc_w, w_out, norm_x_pre, norm_x_post, norm_mem,
           wq, wkv, wo, norm_ffn_pre, norm_ffn_post, w_gu, w_down):
    bsz, seq, d = x.shape
    assert bsz == 1, "causal-conv history is carried across row tiles of one sequence"
    layers = w_in.shape[0]
    d_ff = w_down.shape[1]
    ff_pad = -d_ff % FF_TILE
    xw = wq.shape[2]

    w_in_b = w_in.astype(BF16)
    w_out_b = w_out.astype(BF16)
    wq_b = wq.astype(BF16)
    wkv_b = wkv.astype(BF16)
    wo_b = wo.astype(BF16)
    col_pad = ((0, 0), (0, 0), (0, ff_pad))
    wg_b = jnp.pad(w_gu[:, :, :d_ff].astype(BF16), col_pad)
    wu_b = jnp.pad(w_gu[:, :, d_ff:].astype(BF16), col_pad)
    wd_b = jnp.pad(w_down, ((0, 0), (0, ff_pad), (0, 0))).astype(BF16)

    row = lambda a: a.reshape(layers, 1, a.shape[-1])
    g_mix_pre, g_mix_post = row(norm_mix_pre), row(norm_mix_post)
    g_x_pre, g_x_post = row(norm_x_pre), row(norm_x_post)
    g_ffn_pre, g_ffn_post = row(norm_ffn_pre), row(norm_ffn_post)
    lng, lnb = row(gmlp_ln_g), row(gmlp_ln_b)
    cb, cng, cnb = row(conf_b), row(conf_ln_g), row(conf_ln_b)
    bsT = jnp.swapaxes(gmlp_b, 1, 2)

    kv = _memory_kv(mem[0], row(norm_mem), wkv_b)
    kT = jnp.swapaxes(kv[:, :, :xw], 1, 2)
    vv = kv[:, :, xw:]

    xs = x[0]
    h = _prenorm(xs, g_mix_pre, 0)
    for l in range(layers):
        z = _proj(h, w_in_b, l, 1024, 1024, "mix_in")
        xs, h = _mixer(z, xs, w_out_b, gmlp_ws, bsT, lng, lnb, conf_w, cb, cng, cnb, sc_w,
                       g_mix_post, g_x_pre, l)
        q = _proj(h, wq_b, l, 1024, 1024, "q_proj")
        xs, h = _attention(q, kT, vv, wo_b, xs, g_x_post, g_ffn_pre, l)
        a = _ffn_up(h, wg_b, wu_b, l)
        if l + 1 < layers:
            xs, h = _ffn_down(a, wd_b, xs, g_ffn_post, g_mix_pre, l, l + 1)
        else:
            (xs,) = _ffn_down(a, wd_b, xs, g_ffn_post, None, l, l)
    return xs[None]
```

```python
import functools
import math

import jax
import jax.numpy as jnp
from jax import lax
from jax.experimental import pallas as pl
from jax.experimental.pallas import tpu as pltpu

EPS = 1e-6
CHUNK = 128
HEAD_DIM = 128
CONF_K = 31
SC_K = 3
X_HEADS = 4
SUBLANES = 8
LANES = 128
TAIL_LANES = 512
CONF_HALO = 32
SC_HALO = 8
ROWS = 32
CONV_LANES = 256
FF_TILE = 512
FF_BLOCK = 256
VMEM_LIMIT = 60 * 1024 * 1024

BF16 = jnp.bfloat16
F32 = jnp.float32


def _params(sem):
    return pltpu.CompilerParams(dimension_semantics=sem, vmem_limit_bytes=VMEM_LIMIT)


def _layer_param(arr, layer):
    tail = arr.shape[1:]
    idx = (layer,) + (0,) * len(tail)
    return pl.BlockSpec((None,) + tail, lambda *_: idx, pipeline_mode=pl.Buffered(1))


def _rms_scale(v):
    return lax.rsqrt(jnp.mean(v * v, axis=-1, keepdims=True) + EPS)


def _layer_norm(v, g, b):
    mu = jnp.mean(v, axis=-1, keepdims=True)
    vc = v - mu
    return vc * lax.rsqrt(jnp.mean(vc * vc, axis=-1, keepdims=True) + EPS) * g + b


def _sigmoid(v):
    return 1.0 / (1.0 + jnp.exp(-v))


def _gelu_tanh(v):
    c = math.sqrt(2.0 / math.pi)
    return 0.5 * v * (1.0 + jnp.tanh(c * (v + 0.044715 * (v * v * v))))


def _row_pieces(n_rows, body, unroll=1, static=False):
    if static:
        for r in range(n_rows // ROWS):
            body(pl.ds(r * ROWS, ROWS))
        return

    def step(r, carry):
        body(pl.ds(pl.multiple_of(r * ROWS, ROWS), ROWS))
        return carry
    lax.fori_loop(0, n_rows // ROWS, step, 0, unroll=unroll)


def _prenorm_kernel(x_ref, g_ref, h_ref, *, tm):
    g = g_ref[...]

    def piece(rows):
        x = x_ref[rows, :]
        h_ref[rows, :] = (x * _rms_scale(x) * g).astype(h_ref.dtype)

    _row_pieces(tm, piece, unroll=2)


def _prenorm(x, g, layer, tm=512):
    m, d = x.shape
    return pl.pallas_call(
        functools.partial(_prenorm_kernel, tm=tm),
        grid=(m // tm,),
        in_specs=[pl.BlockSpec((tm, d), lambda i: (i, 0)), _layer_param(g, layer)],
        out_specs=pl.BlockSpec((tm, d), lambda i: (i, 0)),
        out_shape=jax.ShapeDtypeStruct((m, d), BF16),
        compiler_params=_params(("arbitrary",)),
        name="prenorm",
    )(x, g)


def _proj_kernel(a_ref, w_ref, o_ref):
    o_ref[...] = jnp.dot(a_ref[...], w_ref[...], preferred_element_type=F32).astype(o_ref.dtype)


def _proj(a, w, layer, tm, tn, name):
    m, k = a.shape
    n = w.shape[2]
    return pl.pallas_call(
        _proj_kernel,
        grid=(m // tm, n // tn),
        in_specs=[pl.BlockSpec((tm, k), lambda i, j: (i, 0)),
                  pl.BlockSpec((None, k, tn), lambda i, j: (layer, 0, j))],
        out_specs=pl.BlockSpec((tm, tn), lambda i, j: (i, j)),
        out_shape=jax.ShapeDtypeStruct((m, n), BF16),
        compiler_params=_params(("arbitrary", "arbitrary")),
        name=name,
    )(a, w)


def _lane_fold(part, v):
    for t in range(v.shape[1] // LANES):
        s = v[:, t * LANES:(t + 1) * LANES]
        part = s if part is None else part + s
    return part


def _rms_from_partial(part, width):
    return lax.rsqrt(jnp.sum(part, axis=-1, keepdims=True) * (1.0 / width) + EPS)


def _residual_tail(y_ref, x_ref, gpost_ref, gnext_ref, xo_ref, h_ref, n_rows, static=False):
    width = y_ref.shape[1]
    blocks = [slice(c, c + TAIL_LANES) for c in range(0, width, TAIL_LANES)]

    def piece(rows):
        part = None
        for cols in blocks:
            y = y_ref[rows, cols]
            part = _lane_fold(part, y * y)
        rs = _rms_from_partial(part, width)
        part = None
        for cols in blocks:
            xn = x_ref[rows, cols] + y_ref[rows, cols] * rs * gpost_ref[:, cols]
            xo_ref[rows, cols] = xn
            if h_ref is not None:
                part = _lane_fold(part, xn * xn)
        if h_ref is not None:
            rs = _rms_from_partial(part, width)
            for cols in blocks:
                h_ref[rows, cols] = (xo_ref[rows, cols] * rs * gnext_ref[:, cols]).astype(h_ref.dtype)

    _row_pieces(n_rows, piece, unroll=2, static=static)


def _tail_specs(x, gpost, gnext, layer, next_layer, tm, row):
    m, d = x.shape
    in_specs = [pl.BlockSpec((tm, d), row), _layer_param(gpost, layer)]
    operands = [x, gpost]
    out_specs = [pl.BlockSpec((tm, d), row)]
    out_shape = [jax.ShapeDtypeStruct((m, d), F32)]
    if gnext is not None:
        in_specs.append(_layer_param(gnext, next_layer))
        operands.append(gnext)
        out_specs.append(pl.BlockSpec((tm, d), row))
        out_shape.append(jax.ShapeDtypeStruct((m, d), BF16))
    return in_specs, operands, out_specs, out_shape


def _causal_conv_piece(src_sc, w_ref, n_taps, halo, base, lanes, init):
    win = src_sc[pl.ds(base, halo + ROWS), lanes]
    acc = init
    for r in range(min(SUBLANES, n_taps)):
        shifted = win if r == 0 else pltpu.roll(win, r, axis=0)
        for q in range((n_taps - 1 - r) // SUBLANES + 1):
            k = n_taps - 1 - (SUBLANES * q + r)
            lo = halo - SUBLANES * q
            term = w_ref[k:k + 1, lanes] * shifted[lo:lo + ROWS, :]
            acc = term if acc is None else acc + term
    return acc


def _mixer_kernel(z_ref, wout_ref, ws_ref, bsT_ref, lng_ref, lnb_ref,
                  cw_ref, cb_ref, cng_ref, cnb_ref, scw_ref, x_ref, gpost_ref, gnext_ref,
                  xo_ref, h_ref,
                  wm_sc, v_sc, hb_sc, yb_sc, hc_sc, cat_new, cat_cur, y_new, y_cur,
                  *, tm, wa, wb, wc):
    i = pl.program_id(0)
    heads = wa // HEAD_DIM
    zb0 = 2 * wa
    zc0 = 2 * wa + 2 * wb

    @pl.when(i == 0)
    def _init():
        hb_sc[0:CONF_HALO, :] = jnp.zeros((CONF_HALO, wb), F32)
        hc_sc[0:SC_HALO, :] = jnp.zeros((SC_HALO, wc), F32)
        cat_new[...] = jnp.zeros(cat_new.shape, BF16)
        y_new[...] = jnp.zeros(y_new.shape, F32)
        row = lax.broadcasted_iota(jnp.int32, (CHUNK, CHUNK), 0)
        col = lax.broadcasted_iota(jnp.int32, (CHUNK, CHUNK), 1)
        for hh in range(heads):
            wm_sc[hh] = jnp.where(col <= row, ws_ref[hh], 0.0).astype(BF16)

    cat_cur[...] = cat_new[...]
    y_cur[...] = y_new[...]

    _residual_tail(y_cur, x_ref, gpost_ref, gnext_ref, xo_ref, h_ref, tm, static=True)

    y_new[...] = jnp.dot(cat_cur[...], wout_ref[...], preferred_element_type=F32)

    lng = lng_ref[...]
    lnb = lnb_ref[...]

    def norm_v(rows):
        v = _gelu_tanh(z_ref[rows, wa:2 * wa].astype(F32))
        v_sc[rows, :] = _layer_norm(v, lng, lnb).astype(BF16)

    _row_pieces(tm, norm_v, static=True)
    for c in range(tm // CHUNK):
        rows = slice(c * CHUNK, (c + 1) * CHUNK)
        for hh in range(heads):
            cols = slice(hh * HEAD_DIM, (hh + 1) * HEAD_DIM)
            mixed = jnp.dot(wm_sc[hh], v_sc[rows, cols], preferred_element_type=F32)
            mixed = mixed + bsT_ref[:, hh:hh + 1]
            u = _gelu_tanh(z_ref[rows, cols].astype(F32))
            cat_new[rows, cols] = (u * mixed).astype(BF16)

    def glu(rows):
        a = z_ref[rows, zb0:zb0 + wb].astype(F32)
        g = z_ref[rows, zb0 + wb:zb0 + 2 * wb].astype(F32)
        hb_sc[pl.ds(rows.start + CONF_HALO, ROWS), :] = a * _sigmoid(g)

    _row_pieces(tm, glu, static=True)

    def conv_b(rows):
        for lb in range(wb // CONV_LANES):
            lanes = slice(lb * CONV_LANES, (lb + 1) * CONV_LANES)
            init = jnp.broadcast_to(cb_ref[:, lanes], (ROWS, CONV_LANES))
            yb_sc[rows, lanes] = _causal_conv_piece(
                hb_sc, cw_ref, CONF_K, CONF_HALO, rows.start, lanes, init)

    _row_pieces(tm, conv_b, static=True)
    hb_sc[0:CONF_HALO, :] = hb_sc[tm:tm + CONF_HALO, :]
    cng = cng_ref[...]
    cnb = cnb_ref[...]

    def norm_b(rows):
        yb = _layer_norm(yb_sc[rows, :], cng, cnb)
        cat_new[rows, wa:wa + wb] = (yb * _sigmoid(yb)).astype(BF16)

    _row_pieces(tm, norm_b, static=True)

    def gate_c(rows):
        cc = z_ref[rows, zc0 + wc:zc0 + 2 * wc].astype(F32)
        hh_ = z_ref[rows, zc0 + 2 * wc:zc0 + 3 * wc].astype(F32)
        hc_sc[pl.ds(rows.start + SC_HALO, ROWS), :] = cc * hh_

    _row_pieces(tm, gate_c, static=True)

    def conv_c(rows):
        for lb in range(wc // CONV_LANES):
            lanes = slice(lb * CONV_LANES, (lb + 1) * CONV_LANES)
            conv = _causal_conv_piece(hc_sc, scw_ref, SC_K, SC_HALO, rows.start, lanes, None)
            bgate = z_ref[rows, zc0 + lb * CONV_LANES:zc0 + (lb + 1) * CONV_LANES].astype(F32)
            cat_new[rows, wa + wb + lb * CONV_LANES:wa + wb + (lb + 1) * CONV_LANES] = (
                bgate * conv).astype(BF16)

    _row_pieces(tm, conv_c, static=True)
    hc_sc[0:SC_HALO, :] = hc_sc[tm:tm + SC_HALO, :]


def _mixer(z, x, w_out, ws, bsT, lng, lnb, cw, cb, cng, cnb, scw, gpost, gnext, layer, tm=128):
    m, d = x.shape
    d_in = z.shape[1]
    wa = ws.shape[1] * HEAD_DIM
    wb = cw.shape[2]
    wc = scw.shape[2]
    assert d_in == 2 * wa + 2 * wb + 3 * wc and w_out.shape[1] == wa + wb + wc
    assert cw.shape[1] == CONF_K and scw.shape[1] == SC_K and tm % CHUNK == 0
    n_tiles = m // tm
    mix_row = lambda i: (jnp.minimum(i, n_tiles - 1), 0)
    tail_row = lambda i: (jnp.clip(i - 2, 0, n_tiles - 1), 0)
    weights = [w_out, ws, bsT, lng, lnb, cw, cb, cng, cnb, scw]
    t_in, t_ops, out_specs, out_shape = _tail_specs(x, gpost, gnext, layer, layer, tm, tail_row)
    return pl.pallas_call(
        functools.partial(_mixer_kernel, tm=tm, wa=wa, wb=wb, wc=wc),
        grid=(n_tiles + 2,),
        in_specs=[pl.BlockSpec((tm, d_in), mix_row)] + [_layer_param(a, layer) for a in weights] + t_in,
        out_specs=out_specs,
        out_shape=out_shape,
        scratch_shapes=[pltpu.VMEM(ws.shape[1:], BF16),
                        pltpu.VMEM((tm, wa), BF16),
                        pltpu.VMEM((CONF_HALO + tm, wb), F32),
                        pltpu.VMEM((tm, wb), F32),
                        pltpu.VMEM((SC_HALO + tm, wc), F32),
                        pltpu.VMEM((tm, wa + wb + wc), BF16),
                        pltpu.VMEM((tm, wa + wb + wc), BF16),
                        pltpu.VMEM((tm, d), F32),
                        pltpu.VMEM((tm, d), F32)],
        compiler_params=_params(("arbitrary",)),
        name="mixer",
    )(z, *weights, *t_ops)


def _kv_kernel(mem_ref, g_ref, w_ref, o_ref):
    mem = mem_ref[...]
    mn = (mem * _rms_scale(mem) * g_ref[...]).astype(BF16)
    o_ref[...] = jnp.dot(mn, w_ref[...], preferred_element_type=F32).astype(o_ref.dtype)


def _memory_kv(mem, g_mem, wkv, tn=1024):
    n_mem, d = mem.shape
    layers, _, n = wkv.shape
    return pl.pallas_call(
        _kv_kernel,
        grid=(layers, n // tn),
        in_specs=[pl.BlockSpec((n_mem, d), lambda l, j: (0, 0)),
                  pl.BlockSpec((None, 1, d), lambda l, j: (l, 0, 0)),
                  pl.BlockSpec((None, d, tn), lambda l, j: (l, 0, j))],
        out_specs=pl.BlockSpec((None, n_mem, tn), lambda l, j: (l, 0, j)),
        out_shape=jax.ShapeDtypeStruct((layers, n_mem, n), BF16),
        compiler_params=_params(("arbitrary", "arbitrary")),
        name="memory_kv",
    )(mem, g_mem, wkv)


def _attn_kernel(q_ref, kT_ref, v_ref, wo_ref, x_ref, gpost_ref, gnext_ref,
                 xo_ref, h_ref, o_sc, y_new, y_cur, *, tm, dh):
    i = pl.program_id(0)

    @pl.when(i == 0)
    def _init():
        y_new[...] = jnp.zeros(y_new.shape, F32)

    y_cur[...] = y_new[...]
    _residual_tail(y_cur, x_ref, gpost_ref, gnext_ref, xo_ref, h_ref, tm, static=True)

    scale = dh ** -0.5
    for hd in range(X_HEADS):
        cols = slice(hd * dh, (hd + 1) * dh)
        s = jnp.dot(q_ref[:, cols], kT_ref[cols, :], preferred_element_type=F32) * scale
        p = jnp.exp(s - jnp.max(s, axis=-1, keepdims=True))
        p = p / jnp.sum(p, axis=-1, keepdims=True)
        o = jnp.dot(p.astype(BF16), v_ref[:, cols], preferred_element_type=F32)
        o_sc[:, cols] = o.astype(BF16)
    y_new[...] = jnp.dot(o_sc[...], wo_ref[...], preferred_element_type=F32)


def _attention(q, kT, v, wo, x, gpost, gnext, layer, tm=256):
    m, d = x.shape
    xw = q.shape[1]
    n_tiles = m // tm
    q_row = lambda i: (jnp.minimum(i, n_tiles - 1), 0)
    tail_row = lambda i: (jnp.maximum(i - 1, 0), 0)
    weights = [kT, v, wo]
    t_in, t_ops, out_specs, out_shape = _tail_specs(x, gpost, gnext, layer, layer, tm, tail_row)
    return pl.pallas_call(
        functools.partial(_attn_kernel, tm=tm, dh=xw // X_HEADS),
        grid=(n_tiles + 1,),
        in_specs=[pl.BlockSpec((tm, xw), q_row)] + [_layer_param(a, layer) for a in weights] + t_in,
        out_specs=out_specs,
        out_shape=out_shape,
        scratch_shapes=[pltpu.VMEM((tm, xw), BF16),
                        pltpu.VMEM((tm, d), F32),
                        pltpu.VMEM((tm, d), F32)],
        compiler_params=_params(("arbitrary",)),
        name="attention",
    )(q, *weights, *t_ops)


def _ffn_up_kernel(h_ref, *refs, n_blocks, per_step):
    w_refs, a_ref = refs[:-1], refs[-1]
    j = pl.program_id(1)
    n_steps = -(-n_blocks // per_step)
    h = h_ref[...]
    for s in range(per_step):
        gate = jnp.dot(h, w_refs[s][...], preferred_element_type=F32)
        up = jnp.dot(h, w_refs[per_step + s][...], preferred_element_type=F32)
        a = gate * _sigmoid(gate) * up
        if (n_steps - 1) * per_step + s >= n_blocks:
            a = jnp.where(j < n_steps - 1, a, 0.0)
        a_ref[:, s * FF_BLOCK:(s + 1) * FF_BLOCK] = a.astype(a_ref.dtype)


def _ffn_up(h, wgu, layer, tm=1024):
    m, d = h.shape
    tm = min(tm, m)
    d_ff = wgu.shape[2] // 2
    assert d_ff % FF_BLOCK == 0
    n_blocks = d_ff // FF_BLOCK
    per_step = FF_TILE // FF_BLOCK
    n_steps = -(-n_blocks // per_step)

    def w_spec(half, s):
        blk = lambda j: half * n_blocks + jnp.minimum(j * per_step + s, n_blocks - 1)
        return pl.BlockSpec((None, d, FF_BLOCK), lambda i, j: (layer, 0, blk(j)))

    w_specs = [w_spec(half, s) for half in range(2) for s in range(per_step)]
    return pl.pallas_call(
        functools.partial(_ffn_up_kernel, n_blocks=n_blocks, per_step=per_step),
        grid=(m // tm, n_steps),
        in_specs=[pl.BlockSpec((tm, d), lambda i, j: (i, 0))] + w_specs,
        out_specs=pl.BlockSpec((tm, FF_TILE), lambda i, j: (i, j)),
        out_shape=jax.ShapeDtypeStruct((m, n_steps * FF_TILE), BF16),
        compiler_params=_params(("arbitrary", "arbitrary")),
        name="ffn_up",
    )(h, *([wgu] * (2 * per_step)))


def _ffn_down_kernel(a_ref, wd_ref, x_ref, gpost_ref, *rest, n_tiles, emit_h):
    if emit_h:
        gnext_ref, xo_ref, h_ref, acc, y_stash = rest
    else:
        (xo_ref, acc, y_stash), gnext_ref, h_ref = rest, None, None
    i = pl.program_id(0)
    k = pl.program_id(1)
    n_chunks, chunk_rows, _ = y_stash.shape
    live = i < n_tiles
    spread = jnp.logical_and(k > 0, k < n_chunks)

    def stash():
        for c in range(n_chunks):
            y_stash[c] = acc[c * chunk_rows:(c + 1) * chunk_rows, :]

    def tail(c):
        _residual_tail(y_stash.at[c], x_ref, gpost_ref, gnext_ref, xo_ref, h_ref,
                       chunk_rows, static=True)

    def partial_sum():
        return jnp.dot(a_ref[...], wd_ref[...], preferred_element_type=F32)

    has_prev = i > 0

    @pl.when(jnp.logical_and(i == 0, k == 0))
    def _first_tile():
        acc[...] = partial_sum()

    @pl.when(jnp.logical_and(jnp.logical_and(live, has_prev), k == 0))
    def _first():
        stash()
        acc[...] = partial_sum()
        tail(0)

    @pl.when(jnp.logical_and(jnp.logical_and(live, has_prev), spread))
    def _early():
        acc[...] += partial_sum()
        tail(k)

    @pl.when(jnp.logical_and(jnp.logical_and(live, k > 0),
                             jnp.logical_or(i == 0, k >= n_chunks)))
    def _late():
        acc[...] += partial_sum()

    @pl.when(jnp.logical_and(jnp.logical_not(live), k == 0))
    def _drain_first():
        stash()
        tail(0)

    @pl.when(jnp.logical_and(jnp.logical_not(live), spread))
    def _drain():
        tail(k)


def _ffn_down(a, wd, x, gpost, gnext, layer, next_layer, tm=1024, tk=FF_TILE, n_chunks=8):
    m, d = x.shape
    kk = a.shape[1]
    tm = min(tm, m)
    n_tiles = m // tm
    k_steps = kk // tk
    n_chunks = min(n_chunks, k_steps)
    chunk_rows = tm // n_chunks
    assert chunk_rows % ROWS == 0
    k_blk = lambda i, k: jnp.where(i < n_tiles, k, k_steps - 1)
    chunk = lambda i, k: (jnp.where(i > 0, (i - 1) * n_chunks + jnp.minimum(k, n_chunks - 1), 0), 0)
    t_in, t_ops, out_specs, out_shape = _tail_specs(x, gpost, gnext, layer, next_layer,
                                                    chunk_rows, chunk)
    return pl.pallas_call(
        functools.partial(_ffn_down_kernel, n_tiles=n_tiles, emit_h=gnext is not None),
        grid=(n_tiles + 1, k_steps),
        in_specs=[pl.BlockSpec((tm, tk), lambda i, k: (jnp.minimum(i, n_tiles - 1), k_blk(i, k))),
                  pl.BlockSpec((None, tk, d), lambda i, k: (layer, k_blk(i, k), 0))] + t_in,
        out_specs=out_specs,
        out_shape=out_shape,
        scratch_shapes=[pltpu.VMEM((tm, d), F32),
                        pltpu.VMEM((n_chunks, chunk_rows, d), F32)],
        compiler_params=_params(("arbitrary", "arbitrary")),
        name="ffn_down",
    )(a, wd, *t_ops)


def kernel(x, mem, norm_mix_pre, norm_mix_post, w_in, gmlp_ws, gmlp_b, gmlp_ln_g, gmlp_ln_b,
           conf_w, conf_b, conf_ln_g, conf_ln_b, sc_w, w_out, norm_x_pre, norm_x_post, norm_mem,
           wq, wkv, wo, norm_ffn_pre, norm_ffn_post, w_gu, w_down):
    bsz, seq, d = x.shape
    assert bsz == 1, "causal-conv history is carried across row tiles of one sequence"
    layers = w_in.shape[0]
    d_ff = w_down.shape[1]
    ff_pad = -d_ff % FF_TILE
    xw = wq.shape[2]

    w_in_b = w_in.astype(BF16)
    w_out_b = w_out.astype(BF16)
    wq_b = wq.astype(BF16)
    wkv_b = wkv.astype(BF16)
    wo_b = wo.astype(BF16)
    wgu_b = w_gu.astype(BF16)
    wd_b = jnp.pad(w_down, ((0, 0), (0, ff_pad), (0, 0))).astype(BF16)

    row = lambda a: a.reshape(layers, 1, a.shape[-1])
    g_mix_pre, g_mix_post = row(norm_mix_pre), row(norm_mix_post)
    g_x_pre, g_x_post = row(norm_x_pre), row(norm_x_post)
    g_ffn_pre, g_ffn_post = row(norm_ffn_pre), row(norm_ffn_post)
    lng, lnb = row(gmlp_ln_g), row(gmlp_ln_b)
    cb, cng, cnb = row(conf_b), row(conf_ln_g), row(conf_ln_b)
    bsT = jnp.swapaxes(gmlp_b, 1, 2)

    kv = _memory_kv(mem[0], row(norm_mem), wkv_b)
    kT = jnp.swapaxes(kv[:, :, :xw], 1, 2)
    vv = kv[:, :, xw:]

    xs = x[0]
    h = _prenorm(xs, g_mix_pre, 0)
    for l in range(layers):
        z = _proj(h, w_in_b, l, 1024, 1024, "mix_in")
        xs, h = _mixer(z, xs, w_out_b, gmlp_ws, bsT, lng, lnb, conf_w, cb, cng, cnb, sc_w,
                       g_mix_post, g_x_pre, l)
        q = _proj(h, wq_b, l, 1024, 1024, "q_proj")
        xs, h = _attention(q, kT, vv, wo_b, xs, g_x_post, g_ffn_pre, l)
        a = _ffn_up(h, wgu_b, l)
        if l + 1 < layers:
            xs, h = _ffn_down(a, wd_b, xs, g_ffn_post, g_mix_pre, l, l + 1)
        else:
            (xs,) = _ffn_down(a, wd_b, xs, g_ffn_post, None, l, l)
    return xs[None]
```

```python
import functools
import math

import jax
import jax.numpy as jnp
from jax import lax
from jax.experimental import pallas as pl
from jax.experimental.pallas import tpu as pltpu

EPS = 1e-6
CHUNK = 128
HEAD_DIM = 128
CONF_K = 31
SC_K = 3
X_HEADS = 4
SUBLANES = 8
LANES = 128
TAIL_LANES = 512
CONF_HALO = 32
SC_HALO = 8
ROWS = 32
CONV_LANES = 128
FF_TILE = 512
FF_BLOCK = 256
VMEM_LIMIT = 60 * 1024 * 1024

BF16 = jnp.bfloat16
F32 = jnp.float32


def _params(sem):
    return pltpu.CompilerParams(dimension_semantics=sem, vmem_limit_bytes=VMEM_LIMIT)


def _layer_param(arr, layer):
    tail = arr.shape[1:]
    idx = (layer,) + (0,) * len(tail)
    return pl.BlockSpec((None,) + tail, lambda *_: idx, pipeline_mode=pl.Buffered(1))


def _rms_scale(v):
    return lax.rsqrt(jnp.mean(v * v, axis=-1, keepdims=True) + EPS)


def _layer_norm(v, g, b):
    mu = jnp.mean(v, axis=-1, keepdims=True)
    vc = v - mu
    return vc * lax.rsqrt(jnp.mean(vc * vc, axis=-1, keepdims=True) + EPS) * g + b


def _sigmoid(v):
    return 1.0 / (1.0 + jnp.exp(-v))


def _gelu_tanh(v):
    c = math.sqrt(2.0 / math.pi)
    return 0.5 * v * (1.0 + jnp.tanh(c * (v + 0.044715 * (v * v * v))))


def _row_pieces(n_rows, body, unroll=1, static=False):
    if static:
        for r in range(n_rows // ROWS):
            body(pl.ds(r * ROWS, ROWS))
        return

    def step(r, carry):
        body(pl.ds(pl.multiple_of(r * ROWS, ROWS), ROWS))
        return carry
    lax.fori_loop(0, n_rows // ROWS, step, 0, unroll=unroll)


def _prenorm_kernel(x_ref, g_ref, h_ref, *, tm):
    g = g_ref[...]

    def piece(rows):
        x = x_ref[rows, :]
        h_ref[rows, :] = (x * _rms_scale(x) * g).astype(h_ref.dtype)

    _row_pieces(tm, piece, unroll=2)


def _prenorm(x, g, layer, tm=512):
    m, d = x.shape
    return pl.pallas_call(
        functools.partial(_prenorm_kernel, tm=tm),
        grid=(m // tm,),
        in_specs=[pl.BlockSpec((tm, d), lambda i: (i, 0)), _layer_param(g, layer)],
        out_specs=pl.BlockSpec((tm, d), lambda i: (i, 0)),
        out_shape=jax.ShapeDtypeStruct((m, d), BF16),
        compiler_params=_params(("arbitrary",)),
        name="prenorm",
    )(x, g)


def _proj_kernel(a_ref, w_ref, o_ref):
    o_ref[...] = jnp.dot(a_ref[...], w_ref[...], preferred_element_type=F32).astype(o_ref.dtype)


def _proj(a, w, layer, tm, tn, name):
    m, k = a.shape
    n = w.shape[2]
    return pl.pallas_call(
        _proj_kernel,
        grid=(m // tm, n // tn),
        in_specs=[pl.BlockSpec((tm, k), lambda i, j: (i, 0)),
                  pl.BlockSpec((None, k, tn), lambda i, j: (layer, 0, j))],
        out_specs=pl.BlockSpec((tm, tn), lambda i, j: (i, j)),
        out_shape=jax.ShapeDtypeStruct((m, n), BF16),
        compiler_params=_params(("arbitrary", "arbitrary")),
        name=name,
    )(a, w)


def _lane_fold(part, v):
    for t in range(v.shape[1] // LANES):
        s = v[:, t * LANES:(t + 1) * LANES]
        part = s if part is None else part + s
    return part


def _rms_from_partial(part, width):
    return lax.rsqrt(jnp.sum(part, axis=-1, keepdims=True) * (1.0 / width) + EPS)


def _residual_tail(y_ref, x_ref, gpost_ref, gnext_ref, xo_ref, h_ref, n_rows, static=False):
    width = y_ref.shape[1]
    blocks = [slice(c, c + TAIL_LANES) for c in range(0, width, TAIL_LANES)]

    def piece(rows):
        part = None
        for cols in blocks:
            y = y_ref[rows, cols]
            part = _lane_fold(part, y * y)
        rs = _rms_from_partial(part, width)
        part = None
        for cols in blocks:
            xn = x_ref[rows, cols] + y_ref[rows, cols] * rs * gpost_ref[:, cols]
            xo_ref[rows, cols] = xn
            if h_ref is not None:
                part = _lane_fold(part, xn * xn)
        if h_ref is not None:
            rs = _rms_from_partial(part, width)
            for cols in blocks:
                h_ref[rows, cols] = (xo_ref[rows, cols] * rs * gnext_ref[:, cols]).astype(h_ref.dtype)

    _row_pieces(n_rows, piece, unroll=2, static=static)


def _tail_specs(x, gpost, gnext, layer, next_layer, tm, row):
    m, d = x.shape
    in_specs = [pl.BlockSpec((tm, d), row), _layer_param(gpost, layer)]
    operands = [x, gpost]
    out_specs = [pl.BlockSpec((tm, d), row)]
    out_shape = [jax.ShapeDtypeStruct((m, d), F32)]
    if gnext is not None:
        in_specs.append(_layer_param(gnext, next_layer))
        operands.append(gnext)
        out_specs.append(pl.BlockSpec((tm, d), row))
        out_shape.append(jax.ShapeDtypeStruct((m, d), BF16))
    return in_specs, operands, out_specs, out_shape


def _causal_conv_piece(src_sc, w_ref, n_taps, halo, base, lanes, init):
    win = src_sc[pl.ds(base, halo + ROWS), lanes]
    acc = init
    for r in range(min(SUBLANES, n_taps)):
        shifted = win if r == 0 else pltpu.roll(win, r, axis=0)
        for q in range((n_taps - 1 - r) // SUBLANES + 1):
            k = n_taps - 1 - (SUBLANES * q + r)
            lo = halo - SUBLANES * q
            term = w_ref[k:k + 1, lanes] * shifted[lo:lo + ROWS, :]
            acc = term if acc is None else acc + term
    return acc


def _mixer_kernel(z_ref, wout_ref, ws_ref, bsT_ref, lng_ref, lnb_ref,
                  cw_ref, cb_ref, cng_ref, cnb_ref, scw_ref, x_ref, gpost_ref, gnext_ref,
                  xo_ref, h_ref,
                  wm_sc, v_sc, hb_sc, yb_sc, hc_sc, cat_new, cat_cur, y_new, y_cur,
                  *, tm, wa, wb, wc):
    i = pl.program_id(0)
    heads = wa // HEAD_DIM
    zb0 = 2 * wa
    zc0 = 2 * wa + 2 * wb

    @pl.when(i == 0)
    def _init():
        hb_sc[0:CONF_HALO, :] = jnp.zeros((CONF_HALO, wb), F32)
        hc_sc[0:SC_HALO, :] = jnp.zeros((SC_HALO, wc), F32)
        cat_new[...] = jnp.zeros(cat_new.shape, BF16)
        y_new[...] = jnp.zeros(y_new.shape, F32)
        row = lax.broadcasted_iota(jnp.int32, (CHUNK, CHUNK), 0)
        col = lax.broadcasted_iota(jnp.int32, (CHUNK, CHUNK), 1)
        for hh in range(heads):
            wm_sc[hh] = jnp.where(col <= row, ws_ref[hh], 0.0).astype(BF16)

    cat_cur[...] = cat_new[...]
    y_cur[...] = y_new[...]

    _residual_tail(y_cur, x_ref, gpost_ref, gnext_ref, xo_ref, h_ref, tm, static=True)

    y_new[...] = jnp.dot(cat_cur[...], wout_ref[...], preferred_element_type=F32)

    lng = lng_ref[...]
    lnb = lnb_ref[...]

    def norm_v(rows):
        v = _gelu_tanh(z_ref[rows, wa:2 * wa].astype(F32))
        v_sc[rows, :] = _layer_norm(v, lng, lnb).astype(BF16)

    _row_pieces(tm, norm_v, static=True)
    for c in range(tm // CHUNK):
        rows = slice(c * CHUNK, (c + 1) * CHUNK)
        for hh in range(heads):
            cols = slice(hh * HEAD_DIM, (hh + 1) * HEAD_DIM)
            mixed = jnp.dot(wm_sc[hh], v_sc[rows, cols], preferred_element_type=F32)
            mixed = mixed + bsT_ref[:, hh:hh + 1]
            u = _gelu_tanh(z_ref[rows, cols].astype(F32))
            cat_new[rows, cols] = (u * mixed).astype(BF16)

    def glu(rows):
        a = z_ref[rows, zb0:zb0 + wb].astype(F32)
        g = z_ref[rows, zb0 + wb:zb0 + 2 * wb].astype(F32)
        hb_sc[pl.ds(rows.start + CONF_HALO, ROWS), :] = a * _sigmoid(g)

    _row_pieces(tm, glu, static=True)

    def conv_b(rows):
        for lb in range(wb // CONV_LANES):
            lanes = slice(lb * CONV_LANES, (lb + 1) * CONV_LANES)
            init = jnp.broadcast_to(cb_ref[:, lanes], (ROWS, CONV_LANES))
            yb_sc[rows, lanes] = _causal_conv_piece(
                hb_sc, cw_ref, CONF_K, CONF_HALO, rows.start, lanes, init)

    _row_pieces(tm, conv_b, static=True)
    hb_sc[0:CONF_HALO, :] = hb_sc[tm:tm + CONF_HALO, :]
    cng = cng_ref[...]
    cnb = cnb_ref[...]

    def norm_b(rows):
        yb = _layer_norm(yb_sc[rows, :], cng, cnb)
        cat_new[rows, wa:wa + wb] = (yb * _sigmoid(yb)).astype(BF16)

    _row_pieces(tm, norm_b, static=True)

    def gate_c(rows):
        cc = z_ref[rows, zc0 + wc:zc0 + 2 * wc].astype(F32)
        hh_ = z_ref[rows, zc0 + 2 * wc:zc0 + 3 * wc].astype(F32)
        hc_sc[pl.ds(rows.start + SC_HALO, ROWS), :] = cc * hh_

    _row_pieces(tm, gate_c, static=True)

    def conv_c(rows):
        for lb in range(wc // CONV_LANES):
            lanes = slice(lb * CONV_LANES, (lb + 1) * CONV_LANES)
            conv = _causal_conv_piece(hc_sc, scw_ref, SC_K, SC_HALO, rows.start, lanes, None)
            bgate = z_ref[rows, zc0 + lb * CONV_LANES:zc0 + (lb + 1) * CONV_LANES].astype(F32)
            cat_new[rows, wa + wb + lb * CONV_LANES:wa + wb + (lb + 1) * CONV_LANES] = (
                bgate * conv).astype(BF16)

    _row_pieces(tm, conv_c, static=True)
    hc_sc[0:SC_HALO, :] = hc_sc[tm:tm + SC_HALO, :]


def _mixer(z, x, w_out, ws, bsT, lng, lnb, cw, cb, cng, cnb, scw, gpost, gnext, layer, tm=128):
    m, d = x.shape
    d_in = z.shape[1]
    wa = ws.shape[1] * HEAD_DIM
    wb = cw.shape[2]
    wc = scw.shape[2]
    assert d_in == 2 * wa + 2 * wb + 3 * wc and w_out.shape[1] == wa + wb + wc
    assert cw.shape[1] == CONF_K and scw.shape[1] == SC_K and tm % CHUNK == 0
    n_tiles = m // tm
    mix_row = lambda i: (jnp.minimum(i, n_tiles - 1), 0)
    tail_row = lambda i: (jnp.clip(i - 2, 0, n_tiles - 1), 0)
    weights = [w_out, ws, bsT, lng, lnb, cw, cb, cng, cnb, scw]
    t_in, t_ops, out_specs, out_shape = _tail_specs(x, gpost, gnext, layer, layer, tm, tail_row)
    return pl.pallas_call(
        functools.partial(_mixer_kernel, tm=tm, wa=wa, wb=wb, wc=wc),
        grid=(n_tiles + 2,),
        in_specs=[pl.BlockSpec((tm, d_in), mix_row)] + [_layer_param(a, layer) for a in weights] + t_in,
        out_specs=out_specs,
        out_shape=out_shape,
        scratch_shapes=[pltpu.VMEM(ws.shape[1:], BF16),
                        pltpu.VMEM((tm, wa), BF16),
                        pltpu.VMEM((CONF_HALO + tm, wb), F32),
                        pltpu.VMEM((tm, wb), F32),
                        pltpu.VMEM((SC_HALO + tm, wc), F32),
                        pltpu.VMEM((tm, wa + wb + wc), BF16),
                        pltpu.VMEM((tm, wa + wb + wc), BF16),
                        pltpu.VMEM((tm, d), F32),
                        pltpu.VMEM((tm, d), F32)],
        compiler_params=_params(("arbitrary",)),
        name="mixer",
    )(z, *weights, *t_ops)


def _kv_kernel(mem_ref, g_ref, w_ref, o_ref):
    mem = mem_ref[...]
    mn = (mem * _rms_scale(mem) * g_ref[...]).astype(BF16)
    o_ref[...] = jnp.dot(mn, w_ref[...], preferred_element_type=F32).astype(o_ref.dtype)


def _memory_kv(mem, g_mem, wkv, tn=1024):
    n_mem, d = mem.shape
    layers, _, n = wkv.shape
    return pl.pallas_call(
        _kv_kernel,
        grid=(layers, n // tn),
        in_specs=[pl.BlockSpec((n_mem, d), lambda l, j: (0, 0)),
                  pl.BlockSpec((None, 1, d), lambda l, j: (l, 0, 0)),
                  pl.BlockSpec((None, d, tn), lambda l, j: (l, 0, j))],
        out_specs=pl.BlockSpec((None, n_mem, tn), lambda l, j: (l, 0, j)),
        out_shape=jax.ShapeDtypeStruct((layers, n_mem, n), BF16),
        compiler_params=_params(("arbitrary", "arbitrary")),
        name="memory_kv",
    )(mem, g_mem, wkv)


def _attn_kernel(q_ref, kT_ref, v_ref, wo_ref, x_ref, gpost_ref, gnext_ref,
                 xo_ref, h_ref, o_sc, y_new, y_cur, *, tm, dh):
    i = pl.program_id(0)

    @pl.when(i == 0)
    def _init():
        y_new[...] = jnp.zeros(y_new.shape, F32)

    y_cur[...] = y_new[...]
    _residual_tail(y_cur, x_ref, gpost_ref, gnext_ref, xo_ref, h_ref, tm, static=True)

    scale = dh ** -0.5
    for hd in range(X_HEADS):
        cols = slice(hd * dh, (hd + 1) * dh)
        s = jnp.dot(q_ref[:, cols], kT_ref[cols, :], preferred_element_type=F32) * scale
        p = jnp.exp(s - jnp.max(s, axis=-1, keepdims=True))
        p = p / jnp.sum(p, axis=-1, keepdims=True)
        o = jnp.dot(p.astype(BF16), v_ref[:, cols], preferred_element_type=F32)
        o_sc[:, cols] = o.astype(BF16)
    y_new[...] = jnp.dot(o_sc[...], wo_ref[...], preferred_element_type=F32)


def _attention(q, kT, v, wo, x, gpost, gnext, layer, tm=256):
    m, d = x.shape
    xw = q.shape[1]
    n_tiles = m // tm
    q_row = lambda i: (jnp.minimum(i, n_tiles - 1), 0)
    tail_row = lambda i: (jnp.maximum(i - 1, 0), 0)
    weights = [kT, v, wo]
    t_in, t_ops, out_specs, out_shape = _tail_specs(x, gpost, gnext, layer, layer, tm, tail_row)
    return pl.pallas_call(
        functools.partial(_attn_kernel, tm=tm, dh=xw // X_HEADS),
        grid=(n_tiles + 1,),
        in_specs=[pl.BlockSpec((tm, xw), q_row)] + [_layer_param(a, layer) for a in weights] + t_in,
        out_specs=out_specs,
        out_shape=out_shape,
        scratch_shapes=[pltpu.VMEM((tm, xw), BF16),
                        pltpu.VMEM((tm, d), F32),
                        pltpu.VMEM((tm, d), F32)],
        compiler_params=_params(("arbitrary",)),
        name="attention",
    )(q, *weights, *t_ops)


def _ffn_up_kernel(h_ref, *refs, n_blocks, per_step):
    w_refs, a_ref = refs[:-1], refs[-1]
    j = pl.program_id(1)
    n_steps = -(-n_blocks // per_step)
    h = h_ref[...]
    for s in range(per_step):
        gate = jnp.dot(h, w_refs[s][...], preferred_element_type=F32)
        up = jnp.dot(h, w_refs[per_step + s][...], preferred_element_type=F32)
        a = gate * _sigmoid(gate) * up
        if (n_steps - 1) * per_step + s >= n_blocks:
            a = jnp.where(j < n_steps - 1, a, 0.0)
        a_ref[:, s * FF_BLOCK:(s + 1) * FF_BLOCK] = a.astype(a_ref.dtype)


def _ffn_up(h, wgu, layer, tm=1024):
    m, d = h.shape
    tm = min(tm, m)
    d_ff = wgu.shape[2] // 2
    assert d_ff % FF_BLOCK == 0
    n_blocks = d_ff // FF_BLOCK
    per_step = FF_TILE // FF_BLOCK
    n_steps = -(-n_blocks // per_step)

    def w_spec(half, s):
        blk = lambda j: half * n_blocks + jnp.minimum(j * per_step + s, n_blocks - 1)
        return pl.BlockSpec((None, d, FF_BLOCK), lambda i, j: (layer, 0, blk(j)))

    w_specs = [w_spec(half, s) for half in range(2) for s in range(per_step)]
    return pl.pallas_call(
        functools.partial(_ffn_up_kernel, n_blocks=n_blocks, per_step=per_step),
        grid=(m // tm, n_steps),
        in_specs=[pl.BlockSpec((tm, d), lambda i, j: (i, 0))] + w_specs,
        out_specs=pl.BlockSpec((tm, FF_TILE), lambda i, j: (i, j)),
        out_shape=jax.ShapeDtypeStruct((m, n_steps * FF_TILE), BF16),
        compiler_params=_params(("arbitrary", "arbitrary")),
        name="ffn_up",
    )(h, *([wgu] * (2 * per_step)))


def _ffn_down_kernel(a_ref, wd_ref, x_ref, gpost_ref, *rest, n_tiles, emit_h):
    if emit_h:
        gnext_ref, xo_ref, h_ref, acc, y_stash = rest
    else:
        (xo_ref, acc, y_stash), gnext_ref, h_ref = rest, None, None
    i = pl.program_id(0)
    k = pl.program_id(1)
    n_chunks, chunk_rows, _ = y_stash.shape
    live = i < n_tiles
    spread = jnp.logical_and(k > 0, k < n_chunks)

    def stash():
        for c in range(n_chunks):
            y_stash[c] = acc[c * chunk_rows:(c + 1) * chunk_rows, :]

    def tail(c):
        _residual_tail(y_stash.at[c], x_ref, gpost_ref, gnext_ref, xo_ref, h_ref,
                       chunk_rows, static=True)

    def partial_sum():
        return jnp.dot(a_ref[...], wd_ref[...], preferred_element_type=F32)

    has_prev = i > 0

    @pl.when(jnp.logical_and(i == 0, k == 0))
    def _first_tile():
        acc[...] = partial_sum()

    @pl.when(jnp.logical_and(jnp.logical_and(live, has_prev), k == 0))
    def _first():
        stash()
        acc[...] = partial_sum()
        tail(0)

    @pl.when(jnp.logical_and(jnp.logical_and(live, has_prev), spread))
    def _early():
        acc[...] += partial_sum()
        tail(k)

    @pl.when(jnp.logical_and(jnp.logical_and(live, k > 0),
                             jnp.logical_or(i == 0, k >= n_chunks)))
    def _late():
        acc[...] += partial_sum()

    @pl.when(jnp.logical_and(jnp.logical_not(live), k == 0))
    def _drain_first():
        stash()
        tail(0)

    @pl.when(jnp.logical_and(jnp.logical_not(live), spread))
    def _drain():
        tail(k)


def _ffn_down(a, wd, x, gpost, gnext, layer, next_layer, tm=1024, tk=FF_TILE, n_chunks=8):
    m, d = x.shape
    kk = a.shape[1]
    tm = min(tm, m)
    n_tiles = m // tm
    k_steps = kk // tk
    n_chunks = min(n_chunks, k_steps)
    chunk_rows = tm // n_chunks
    assert chunk_rows % ROWS == 0
    k_blk = lambda i, k: jnp.where(i < n_tiles, k, k_steps - 1)
    chunk = lambda i, k: (jnp.where(i > 0, (i - 1) * n_chunks + jnp.minimum(k, n_chunks - 1), 0), 0)
    t_in, t_ops, out_specs, out_shape = _tail_specs(x, gpost, gnext, layer, next_layer,
                                                    chunk_rows, chunk)
    return pl.pallas_call(
        functools.partial(_ffn_down_kernel, n_tiles=n_tiles, emit_h=gnext is not None),
        grid=(n_tiles + 1, k_steps),
        in_specs=[pl.BlockSpec((tm, tk), lambda i, k: (jnp.minimum(i, n_tiles - 1), k_blk(i, k))),
                  pl.BlockSpec((None, tk, d), lambda i, k: (layer, k_blk(i, k), 0))] + t_in,
        out_specs=out_specs,
        out_shape=out_shape,
        scratch_shapes=[pltpu.VMEM((tm, d), F32),
                        pltpu.VMEM((n_chunks, chunk_rows, d), F32)],
        compiler_params=_params(("arbitrary", "arbitrary")),
        name="ffn_down",
    )(a, wd, *t_ops)


def kernel(x, mem, norm_mix_pre, norm_mix_post, w_in, gmlp_ws, gmlp_b, gmlp_ln_g, gmlp_ln_b,
           conf_w, conf_b, conf_ln_g, conf_ln_b, sc_w, w_out, norm_x_pre, norm_x_post, norm_mem,
           wq, wkv, wo, norm_ffn_pre, norm_ffn_post, w_gu, w_down):
    bsz, seq, d = x.shape
    assert bsz == 1, "causal-conv history is carried across row tiles of one sequence"
    layers = w_in.shape[0]
    d_ff = w_down.shape[1]
    ff_pad = -d_ff % FF_TILE
    xw = wq.shape[2]

    w_in_b = w_in.astype(BF16)
    w_out_b = w_out.astype(BF16)
    wq_b = wq.astype(BF16)
    wkv_b = wkv.astype(BF16)
    wo_b = wo.astype(BF16)
    wgu_b = w_gu.astype(BF16)
    wd_b = jnp.pad(w_down, ((0, 0), (0, ff_pad), (0, 0))).astype(BF16)

    row = lambda a: a.reshape(layers, 1, a.shape[-1])
    g_mix_pre, g_mix_post = row(norm_mix_pre), row(norm_mix_post)
    g_x_pre, g_x_post = row(norm_x_pre), row(norm_x_post)
    g_ffn_pre, g_ffn_post = row(norm_ffn_pre), row(norm_ffn_post)
    lng, lnb = row(gmlp_ln_g), row(gmlp_ln_b)
    cb, cng, cnb = row(conf_b), row(conf_ln_g), row(conf_ln_b)
    bsT = jnp.swapaxes(gmlp_b, 1, 2)

    kv = _memory_kv(mem[0], row(norm_mem), wkv_b)
    kT = jnp.swapaxes(kv[:, :, :xw], 1, 2)
    vv = kv[:, :, xw:]

    xs = x[0]
    h = _prenorm(xs, g_mix_pre, 0)
    for l in range(layers):
        z = _proj(h, w_in_b, l, 1024, 1024, "mix_in")
        xs, h = _mixer(z, xs, w_out_b, gmlp_ws, bsT, lng, lnb, conf_w, cb, cng, cnb, sc_w,
                       g_mix_post, g_x_pre, l)
        q = _proj(h, wq_b, l, 1024, 1024, "q_proj")
        xs, h = _attention(q, kT, vv, wo_b, xs, g_x_post, g_ffn_pre, l)
        a = _ffn_up(h, wgu_b, l)
        if l + 1 < layers:
            xs, h = _ffn_down(a, wd_b, xs, g_ffn_post, g_mix_pre, l, l + 1)
        else:
            (xs,) = _ffn_down(a, wd_b, xs, g_ffn_post, None, l, l)
    return xs[None]
```
